```python
import jax, jax.numpy as jnp
from jax import lax
import numpy as np

D_MODEL = 4096
BATCH = 2
SEQ = 8192
DEPTH = 2

CHUNK = 64
Q_BLOCK = 128
EPS = 1e-6

CONV_DIM = 1024
CONV_WIDTH = 31
SG_DIM = 1024
SG_BLOCK = 128
SG_GROUPS = 8
SG_GROUP_DIM = SG_DIM // SG_GROUPS
MLA_HEADS = 16
Q_LORA = 1024
KV_LORA = 512
QK_NOPE = 128
QK_ROPE = 64
V_DIM = 128
QK_DIM = QK_NOPE + QK_ROPE
ROPE_BASE = 10000.0
N_BRANCH = 3
IN_COLS = 2 * CONV_DIM + 2 * SG_DIM + Q_LORA + KV_LORA + QK_ROPE + N_BRANCH * D_MODEL
N_GROUPS = 4
EXPERTS_PER_GROUP = 8
N_EXPERTS = N_GROUPS * EXPERTS_PER_GROUP
TOP_K = 2
D_EXPERT = 768

kernel_name = 'hybrid_streaming_encoder_block'


def rms_norm(x, g):
    xf = x.astype(jnp.float32)
    y = xf * lax.rsqrt(jnp.mean(xf * xf, axis=-1, keepdims=True) + EPS)
    return (y * g.astype(jnp.float32)).astype(x.dtype)


def rope_tables(positions):
    inv = jnp.power(ROPE_BASE, -jnp.arange(0, QK_ROPE, 2, dtype=jnp.float32) / QK_ROPE)
    ang = positions.astype(jnp.float32)[..., None] * inv
    return jnp.cos(ang), jnp.sin(ang)


def apply_rope(x, cos, sin):
    x1, x2 = jnp.split(x, 2, axis=-1)
    cos = cos.astype(x.dtype)
    sin = sin.astype(x.dtype)
    return jnp.concatenate([x1 * cos - x2 * sin, x1 * sin + x2 * cos], axis=-1)


def conformer_conv(a, conv_w, conv_b, conv_norm, w_a_out):
    a_lin, a_gate = jnp.split(a, 2, axis=-1)
    u = a_lin * jax.nn.sigmoid(a_gate)
    u = lax.conv_general_dilated(
        u, conv_w[:, None, :], window_strides=(1,), padding=[(CONV_WIDTH - 1, 0)],
        dimension_numbers=('NWC', 'WIO', 'NWC'), feature_group_count=CONV_DIM) + conv_b
    u = jax.nn.silu(rms_norm(u, conv_norm))
    return u @ w_a_out


def spatial_gating(z, sg_norm, sg_w, sg_b, w_b_out):
    b, s, _ = z.shape
    z = jax.nn.gelu(z)
    u, v = jnp.split(z, 2, axis=-1)
    v = rms_norm(v, sg_norm).reshape(b, s // SG_BLOCK, SG_BLOCK, SG_GROUPS, SG_GROUP_DIM)
    pos_chunk = jnp.arange(SG_BLOCK) // CHUNK
    mask = pos_chunk[:, None] >= pos_chunk[None, :]
    w = jnp.where(mask[None], sg_w, 0.0)
    sv = jnp.einsum('gij,bnjgc->bnigc', w, v) + jnp.transpose(sg_b)[None, None, :, :, None]
    return (u * sv.reshape(b, s, SG_DIM)) @ w_b_out


def latent_attention(c_q, c_kv, k_rope, cos, sin, q_norm, w_uq, kv_norm, w_ukv, q_gain, k_gain, w_c_out):
    b, s, _ = c_q.shape
    q = (rms_norm(c_q, q_norm) @ w_uq).reshape(b, s, MLA_HEADS, QK_DIM)
    kv = (rms_norm(c_kv, kv_norm) @ w_ukv).reshape(b, s, MLA_HEADS, QK_NOPE + V_DIM)
    k_nope, v = kv[..., :QK_NOPE], kv[..., QK_NOPE:]
    q_nope = rms_norm(q[..., :QK_NOPE], q_gain[:QK_NOPE])
    q_rope = apply_rope(rms_norm(q[..., QK_NOPE:], q_gain[QK_NOPE:]), cos[:, :, None], sin[:, :, None])
    k_nope = rms_norm(k_nope, k_gain[:QK_NOPE])
    k_rope = apply_rope(rms_norm(k_rope, k_gain[QK_NOPE:]), cos, sin)
    scale = QK_DIM ** -0.5
    chunk_id = jnp.arange(s) // CHUNK
    outs = []
    for blk in range(s // Q_BLOCK):
        q0 = blk * Q_BLOCK
        kend = q0 + Q_BLOCK
        scores = (jnp.einsum('bqhd,bkhd->bhqk', q_nope[:, q0:kend], k_nope[:, :kend])
                  + jnp.einsum('bqhr,bkr->bhqk', q_rope[:, q0:kend], k_rope[:, :kend])).astype(jnp.float32) * scale
        mask = chunk_id[q0:kend, None] >= chunk_id[None, :kend]
        p = jax.nn.softmax(jnp.where(mask, scores, -jnp.inf), axis=-1).astype(v.dtype)
        outs.append(jnp.einsum('bhqk,bkhd->bqhd', p, v[:, :kend]))
    o = jnp.concatenate(outs, axis=1).reshape(b, s, MLA_HEADS * V_DIM)
    return o @ w_c_out


def mixer_block(x, cos, sin, mix_norm, w_in, conv_w, conv_b, conv_norm, w_a_out,
                sg_norm, sg_w, sg_b, w_b_out, q_norm, w_uq, kv_norm, w_ukv, q_gain, k_gain,
                w_c_out, w_out):
    b, s, _ = x.shape
    h = rms_norm(x, mix_norm)
    z = h @ w_in
    i0 = 2 * CONV_DIM
    i1 = i0 + 2 * SG_DIM
    i2 = i1 + Q_LORA
    i3 = i2 + KV_LORA
    i4 = i3 + QK_ROPE
    a, zb, c_q, c_kv, k_rope, gates = jnp.split(z, [i0, i1, i2, i3, i4], axis=-1)
    y_a = conformer_conv(a, conv_w, conv_b, conv_norm, w_a_out)
    y_b = spatial_gating(zb, sg_norm, sg_w, sg_b, w_b_out)
    y_c = latent_attention(c_q, c_kv, k_rope, cos, sin, q_norm, w_uq, kv_norm, w_ukv, q_gain, k_gain, w_c_out)
    g = jax.nn.sigmoid(gates.reshape(b, s, N_BRANCH, D_MODEL))
    y = g[:, :, 0] * y_a + g[:, :, 1] * y_b + g[:, :, 2] * y_c
    return y @ w_out


def hier_moe(h, w_rg, b_rg, w_re, b_re, w_gate, w_up, w_down):
    b, s, d = h.shape
    t = h.reshape(b * s, d)
    n_tok = b * s
    g_prob = jax.nn.softmax((t @ w_rg).astype(jnp.float32) + b_rg.astype(jnp.float32), axis=-1)
    g_w, g_idx = lax.top_k(g_prob, 1)
    g_idx = g_idx[:, 0]
    e_logits = ((t @ w_re).astype(jnp.float32) + b_re.astype(jnp.float32)).reshape(n_tok, N_GROUPS, EXPERTS_PER_GROUP)
    e_logits = e_logits[jnp.arange(n_tok), g_idx]
    e_prob = jax.nn.softmax(e_logits, axis=-1)
    top_p, top_i = lax.top_k(e_prob, TOP_K)
    top_p = top_p / jnp.sum(top_p, axis=-1, keepdims=True)
    expert_id = g_idx[:, None] * EXPERTS_PER_GROUP + top_i
    weight = g_w * top_p
    combine = jnp.sum(jax.nn.one_hot(expert_id, N_EXPERTS, dtype=jnp.float32) * weight[..., None], axis=1).astype(t.dtype)
    out = jnp.zeros_like(t)
    for e in range(N_EXPERTS):
        hid = jax.nn.silu(t @ w_gate[e]) * (t @ w_up[e])
        out = out + (hid @ w_down[e]) * combine[:, e:e + 1]
    return out.reshape(b, s, d)


def setup_inputs(seed: int = 0) -> dict:
    key = jax.random.key(seed)
    ks = jax.random.split(key, 32)
    L = DEPTH
    f32 = jnp.float32

    def nrm(k, shape, fan_in):
        return jax.random.normal(k, shape, f32) * (fan_in ** -0.5)

    def gain(k, shape):
        return 1.0 + 0.01 * jax.random.normal(k, shape, f32)

    def small(k, shape):
        return 0.01 * jax.random.normal(k, shape, f32)

    x = jax.random.normal(ks[0], (BATCH, SEQ, D_MODEL), f32)
    offset = jax.random.randint(ks[1], (BATCH, 1), 0, 4096, dtype=jnp.int32)
    positions = offset + jnp.arange(SEQ, dtype=jnp.int32)[None, :]
    return {
        'x': x,
        'positions': positions,
        'mix_norm': gain(ks[2], (L, D_MODEL)),
        'w_in': nrm(ks[3], (L, D_MODEL, IN_COLS), D_MODEL),
        'conv_w': nrm(ks[4], (L, CONV_WIDTH, CONV_DIM), CONV_WIDTH),
        'conv_b': small(ks[5], (L, CONV_DIM)),
        'conv_norm': gain(ks[6], (L, CONV_DIM)),
        'w_a_out': nrm(ks[7], (L, CONV_DIM, D_MODEL), CONV_DIM),
        'sg_norm': gain(ks[8], (L, SG_DIM)),
        'sg_w': nrm(ks[9], (L, SG_GROUPS, SG_BLOCK, SG_BLOCK), SG_BLOCK),
        'sg_b': 1.0 + 0.1 * jax.random.normal(ks[10], (L, SG_GROUPS, SG_BLOCK), f32),
        'w_b_out': nrm(ks[11], (L, SG_DIM, D_MODEL), SG_DIM),
        'q_norm': gain(ks[12], (L, Q_LORA)),
        'w_uq': nrm(ks[13], (L, Q_LORA, MLA_HEADS * QK_DIM), Q_LORA),
        'kv_norm': gain(ks[14], (L, KV_LORA)),
        'w_ukv': nrm(ks[15], (L, KV_LORA, MLA_HEADS * (QK_NOPE + V_DIM)), KV_LORA),
        'q_gain': gain(ks[16], (L, QK_DIM)),
        'k_gain': gain(ks[17], (L, QK_DIM)),
        'w_c_out': nrm(ks[18], (L, MLA_HEADS * V_DIM, D_MODEL), MLA_HEADS * V_DIM),
        'w_out': nrm(ks[19], (L, D_MODEL, D_MODEL), D_MODEL),
        'ffn_norm': gain(ks[20], (L, D_MODEL)),
        'w_router_group': nrm(ks[21], (L, D_MODEL, N_GROUPS), D_MODEL),
        'b_router_group': small(ks[22], (L, N_GROUPS)),
        'w_router_expert': nrm(ks[23], (L, D_MODEL, N_EXPERTS), D_MODEL),
        'b_router_expert': small(ks[24], (L, N_EXPERTS)),
        'w_gate': nrm(ks[25], (L, N_EXPERTS, D_MODEL, D_EXPERT), D_MODEL),
        'w_up': nrm(ks[26], (L, N_EXPERTS, D_MODEL, D_EXPERT), D_MODEL),
        'w_down': nrm(ks[27], (L, N_EXPERTS, D_EXPERT, D_MODEL), D_EXPERT),
    }


def reference(x, positions, mix_norm, w_in, conv_w, conv_b, conv_norm, w_a_out,
              sg_norm, sg_w, sg_b, w_b_out, q_norm, w_uq, kv_norm, w_ukv, q_gain, k_gain,
              w_c_out, w_out, ffn_norm, w_router_group, b_router_group, w_router_expert,
              b_router_expert, w_gate, w_up, w_down):
    cos, sin = rope_tables(positions)
    for l in range(DEPTH):
        x = x + mixer_block(x, cos, sin, mix_norm[l], w_in[l], conv_w[l], conv_b[l], conv_norm[l], w_a_out[l],
                            sg_norm[l], sg_w[l], sg_b[l], w_b_out[l], q_norm[l], w_uq[l], kv_norm[l], w_ukv[l],
                            q_gain[l], k_gain[l], w_c_out[l], w_out[l])
        x = x + hier_moe(rms_norm(x, ffn_norm[l]), w_router_group[l], b_router_group[l], w_router_expert[l],
                         b_router_expert[l], w_gate[l], w_up[l], w_down[l])
    return x
```

```python
import functools

import jax
import jax.numpy as jnp
from jax import lax
from jax.experimental import pallas as pl
from jax.experimental.pallas import tpu as pltpu

CHUNK = 64
EPS = 1e-6
QK_NOPE = 128
QK_ROPE = 64
V_DIM = 128
QK_DIM = QK_NOPE + QK_ROPE
ROPE_BASE = 10000.0
TOP_K = 2

LANES = 128
HEAD_PAD = 2 * LANES
VMEM_LIMIT_BYTES = 56 * 1024 * 1024
CONV_HIST = 32
NEG_BIG = -1e30

F32 = jnp.float32
BF16 = jnp.bfloat16


def _tiles(n, pref):
    t = min(n, pref)
    while n % t:
        t //= 2
    return t


def _params(*sem):
    return pltpu.CompilerParams(dimension_semantics=sem, vmem_limit_bytes=VMEM_LIMIT_BYTES)


def _resident(shape):
    nd = len(shape)
    return pl.BlockSpec(shape, lambda *_: (0,) * nd, pipeline_mode=pl.Buffered(1))


def _rms(xf, g):
    ms = jnp.mean(xf * xf, axis=-1, keepdims=True)
    return xf * lax.rsqrt(ms + EPS) * g


def _sigmoid(x):
    return 1.0 / (1.0 + jnp.exp(-x))


def _rmsnorm_kernel(x_ref, g_ref, o_ref):
    o_ref[...] = _rms(x_ref[...].astype(F32), g_ref[...]).astype(o_ref.dtype)


def rmsnorm_rows(x, g, out_dtype=BF16, tm=512):
    t, d = x.shape
    tm = _tiles(t, tm)
    return pl.pallas_call(
        _rmsnorm_kernel,
        grid=(t // tm,),
        in_specs=[pl.BlockSpec((tm, d), lambda i: (i, 0)), _resident((1, d))],
        out_specs=pl.BlockSpec((tm, d), lambda i: (i, 0)),
        out_shape=jax.ShapeDtypeStruct((t, d), out_dtype),
        compiler_params=_params("parallel"),
        name="rmsnorm_rows",
    )(x, g.reshape(1, d))


def _matmul_kernel(a_ref, b_ref, o_ref):
    o_ref[...] = jnp.dot(a_ref[...], b_ref[...], preferred_element_type=F32).astype(o_ref.dtype)


def _matmul_res_kernel(a_ref, b_ref, r_ref, o_ref):
    acc = jnp.dot(a_ref[...], b_ref[...], preferred_element_type=F32)
    o_ref[...] = (r_ref[...].astype(F32) + acc).astype(o_ref.dtype)


def matmul(a, b, out_dtype, residual=None, tm=1024, tn=1024, name="matmul"):
    m, k = a.shape
    n = b.shape[1]
    tm, tn = _tiles(m, tm), _tiles(n, tn)
    in_specs = [pl.BlockSpec((tm, k), lambda i, j: (i, 0)), pl.BlockSpec((k, tn), lambda i, j: (0, j))]
    args = [a, b]
    kern = _matmul_kernel
    if residual is not None:
        in_specs.append(pl.BlockSpec((tm, tn), lambda i, j: (i, j)))
        args.append(residual)
        kern = _matmul_res_kernel
    return pl.pallas_call(
        kern,
        grid=(m // tm, n // tn),
        in_specs=in_specs,
        out_specs=pl.BlockSpec((tm, tn), lambda i, j: (i, j)),
        out_shape=jax.ShapeDtypeStruct((m, n), out_dtype),
        compiler_params=_params("parallel", "arbitrary"),
        name=name,
    )(*args)


def _mixer_a_kernel(lin_ref, gate_ref, cw_ref, cb_ref, cn_ref, wout_ref, g_ref, o_ref, ext_ref, conv_ref,
                    *, ts, width, rows_per_step):
    c = lin_ref.shape[-1]
    off = CONV_HIST - (width - 1)

    @pl.when(pl.program_id(1) == 0)
    def _():
        ext_ref[0:CONV_HIST, :] = jnp.zeros((CONV_HIST, c), F32)

    ext_ref[CONV_HIST:CONV_HIST + ts, :] = lin_ref[...].astype(F32) * _sigmoid(gate_ref[...].astype(F32))

    for r0 in range(0, ts, rows_per_step):
        acc = jnp.broadcast_to(cb_ref[...], (rows_per_step, c))
        for k in range(width):
            acc = acc + cw_ref[k:k + 1, :] * ext_ref[r0 + off + k:r0 + off + k + rows_per_step, :]
        conv_ref[r0:r0 + rows_per_step, :] = acc

    ext_ref[0:CONV_HIST, :] = ext_ref[ts:ts + CONV_HIST, :]

    y = _rms(conv_ref[...], cn_ref[...])
    y = y * _sigmoid(y)
    out = jnp.dot(y.astype(BF16), wout_ref[...], preferred_element_type=F32)
    o_ref[...] = (_sigmoid(g_ref[...].astype(F32)) * out).astype(o_ref.dtype)


def mixer_a(z_main, gates, conv_w, conv_b, conv_norm, w_a_out, batch, seq, lin_blk, gate_blk, g_blk, ts=256):
    width, c = conv_w.shape
    d = w_a_out.shape[1]
    ts = _tiles(seq, ts)
    nt = seq // ts
    rows_per_step = min(32, ts)
    kern = functools.partial(_mixer_a_kernel, ts=ts, width=width, rows_per_step=rows_per_step)
    row = lambda b, t: b * nt + t
    return pl.pallas_call(
        kern,
        grid=(batch, nt),
        in_specs=[
            pl.BlockSpec((ts, c), lambda b, t: (row(b, t), lin_blk)),
            pl.BlockSpec((ts, c), lambda b, t: (row(b, t), gate_blk)),
            _resident((width, c)), _resident((1, c)), _resident((1, c)),
            _resident((c, d)),
            pl.BlockSpec((ts, d), lambda b, t: (row(b, t), g_blk)),
        ],
        out_specs=pl.BlockSpec((ts, d), lambda b, t: (row(b, t), 0)),
        out_shape=jax.ShapeDtypeStruct((batch * seq, d), BF16),
        scratch_shapes=[pltpu.VMEM((ts + CONV_HIST, c), F32), pltpu.VMEM((ts, c), F32)],
        compiler_params=_params("arbitrary", "arbitrary"),
        name="mixer_a",
    )(z_main, z_main, conv_w, conv_b.reshape(1, c), conv_norm.reshape(1, c), w_a_out, gates)


def _gelu_tanh(x):
    return 0.5 * x * (1.0 + jnp.tanh(0.7978845608028654 * (x + 0.044715 * (x * x * x))))


def _mixer_b_kernel(u_ref, v_ref, sn_ref, sw_ref, sb_ref, wout_ref, g_ref, yin_ref, o_ref, uv_ref,
                    *, n_blocks, groups, block):
    gd = u_ref.shape[-1] // groups
    v = _rms(_gelu_tanh(v_ref[...].astype(F32)), sn_ref[...]).astype(BF16)
    rc = lax.broadcasted_iota(jnp.int32, (block, block), 0) // CHUNK
    cc = lax.broadcasted_iota(jnp.int32, (block, block), 1) // CHUNK
    causal = rc >= cc
    for g in range(groups):
        w = jnp.where(causal, sw_ref[g], 0.0).astype(BF16)
        bias = sb_ref[:, g:g + 1]
        cols = slice(g * gd, (g + 1) * gd)
        for r in range(n_blocks):
            rows = slice(r * block, (r + 1) * block)
            sv = jnp.dot(w, v[rows, cols], preferred_element_type=F32) + bias
            u = _gelu_tanh(u_ref[rows, cols].astype(F32))
            uv_ref[rows, cols] = (u * sv).astype(BF16)
    out = jnp.dot(uv_ref[...], wout_ref[...], preferred_element_type=F32)
    o_ref[...] = (yin_ref[...].astype(F32) + _sigmoid(g_ref[...].astype(F32)) * out).astype(o_ref.dtype)


def mixer_b(z_main, gates, y_in, sg_norm, sg_w, sg_b, w_b_out, u_blk, v_blk, g_blk, ts=512):
    groups, block, _ = sg_w.shape
    c, d = w_b_out.shape
    t = z_main.shape[0]
    ts = max(_tiles(t, ts), block)
    kern = functools.partial(_mixer_b_kernel, n_blocks=ts // block, groups=groups, block=block)
    return pl.pallas_call(
        kern,
        grid=(t // ts,),
        in_specs=[
            pl.BlockSpec((ts, c), lambda i: (i, u_blk)),
            pl.BlockSpec((ts, c), lambda i: (i, v_blk)),
            _resident((1, c)), _resident((groups, block, block)), _resident((block, groups)),
            _resident((c, d)),
            pl.BlockSpec((ts, d), lambda i: (i, g_blk)),
            pl.BlockSpec((ts, d), lambda i: (i, 0)),
        ],
        out_specs=pl.BlockSpec((ts, d), lambda i: (i, 0)),
        out_shape=jax.ShapeDtypeStruct((t, d), BF16),
        scratch_shapes=[pltpu.VMEM((ts, c), BF16)],
        compiler_params=_params("parallel"),
        name="mixer_b",
    )(z_main, z_main, sg_norm.reshape(1, c), sg_w, jnp.transpose(sg_b), w_b_out, gates, y_in)


def _rope_group(x, cos_ref, sa_ref, sb_ref):
    half = QK_ROPE // 2
    return (x * cos_ref[...] + pltpu.roll(x, LANES - half, 1) * sa_ref[...]
            + pltpu.roll(x, half, 1) * sb_ref[...])


def _norm_group(x, gain, n):
    ms = jnp.sum(x * x, axis=-1, keepdims=True) * (1.0 / n)
    return x * lax.rsqrt(ms + EPS) * gain


def _q_proj_kernel(cq_ref, qn_ref, wq_ref, gn_ref, gr_ref, cos_ref, sa_ref, sb_ref, o_ref, *, heads):
    h = _rms(cq_ref[...].astype(F32), qn_ref[...]).astype(BF16)
    q = jnp.dot(h, wq_ref[...], preferred_element_type=F32)
    scale = QK_DIM ** -0.5
    for hd in range(heads):
        c0 = hd * HEAD_PAD
        nope = _norm_group(q[:, c0:c0 + QK_NOPE], gn_ref[...], QK_NOPE)
        rope = _norm_group(q[:, c0 + QK_NOPE:c0 + HEAD_PAD], gr_ref[...], QK_ROPE)
        rope = _rope_group(rope, cos_ref, sa_ref, sb_ref)
        o_ref[:, c0:c0 + QK_NOPE] = (nope * scale).astype(o_ref.dtype)
        o_ref[:, c0 + QK_NOPE:c0 + HEAD_PAD] = (rope * scale).astype(o_ref.dtype)


def _kv_proj_kernel(ckv_ref, kr_ref, kvn_ref, wkv_ref, gn_ref, gr_ref, cos_ref, sa_ref, sb_ref,
                    k_ref, v_ref, *, heads):
    h = _rms(ckv_ref[...].astype(F32), kvn_ref[...]).astype(BF16)
    kv = jnp.dot(h, wkv_ref[...], preferred_element_type=F32)
    rope = _norm_group(kr_ref[...].astype(F32), gr_ref[...], QK_ROPE)
    rope = _rope_group(rope, cos_ref, sa_ref, sb_ref).astype(k_ref.dtype)
    for hd in range(heads):
        c0 = hd * (QK_NOPE + V_DIM)
        nope = _norm_group(kv[:, c0:c0 + QK_NOPE], gn_ref[...], QK_NOPE)
        k_ref[:, hd * HEAD_PAD:hd * HEAD_PAD + QK_NOPE] = nope.astype(k_ref.dtype)
        k_ref[:, hd * HEAD_PAD + QK_NOPE:(hd + 1) * HEAD_PAD] = rope
        v_ref[:, hd * V_DIM:(hd + 1) * V_DIM] = kv[:, c0 + QK_NOPE:c0 + QK_NOPE + V_DIM].astype(v_ref.dtype)


def _pad_gain(gain):
    gn = gain[:QK_NOPE].reshape(1, QK_NOPE)
    gr = jnp.concatenate([gain[QK_NOPE:], jnp.zeros((LANES - QK_ROPE,), gain.dtype)]).reshape(1, LANES)
    return gn, gr


def q_proj(z_main, cq_blk, q_norm, w_uq_pad, q_gain, tabs, heads, tm=256):
    t = z_main.shape[0]
    ql = q_norm.shape[0]
    tm = _tiles(t, tm)
    gn, gr = _pad_gain(q_gain)
    tab_spec = pl.BlockSpec((tm, LANES), lambda i: (i, 0))
    return pl.pallas_call(
        functools.partial(_q_proj_kernel, heads=heads),
        grid=(t // tm,),
        in_specs=[pl.BlockSpec((tm, ql), lambda i: (i, cq_blk)), _resident((1, ql)),
                  _resident((ql, heads * HEAD_PAD)), _resident((1, QK_NOPE)), _resident((1, LANES)),
                  tab_spec, tab_spec, tab_spec],
        out_specs=pl.BlockSpec((tm, heads * HEAD_PAD), lambda i: (i, 0)),
        out_shape=jax.ShapeDtypeStruct((t, heads * HEAD_PAD), BF16),
        compiler_params=_params("parallel"),
        name="q_proj",
    )(z_main, q_norm.reshape(1, ql), w_uq_pad, gn, gr, *tabs)


def kv_proj(z_main, ckv_blk, k_rope, kv_norm, w_ukv, k_gain, tabs, heads, tm=256):
    t = z_main.shape[0]
    kl = kv_norm.shape[0]
    tm = _tiles(t, tm)
    gn, gr = _pad_gain(k_gain)
    tab_spec = pl.BlockSpec((tm, LANES), lambda i: (i, 0))
    return pl.pallas_call(
        functools.partial(_kv_proj_kernel, heads=heads),
        grid=(t // tm,),
        in_specs=[pl.BlockSpec((tm, kl), lambda i: (i, ckv_blk)), tab_spec, _resident((1, kl)),
                  _resident((kl, heads * (QK_NOPE + V_DIM))), _resident((1, QK_NOPE)), _resident((1, LANES)),
                  tab_spec, tab_spec, tab_spec],
        out_specs=[pl.BlockSpec((tm, heads * HEAD_PAD), lambda i: (i, 0)),
                   pl.BlockSpec((tm, heads * V_DIM), lambda i: (i, 0))],
        out_shape=[jax.ShapeDtypeStruct((t, heads * HEAD_PAD), BF16),
                   jax.ShapeDtypeStruct((t, heads * V_DIM), BF16)],
        compiler_params=_params("parallel"),
        name="kv_proj",
    )(z_main, k_rope, kv_norm.reshape(1, kl), w_ukv, gn, gr, *tabs)


def _attn_kernel(q_ref, k_ref, v_ref, o_ref, *, tq):
    i = pl.program_id(2)
    q = q_ref[...]

    def step(j, carry, diagonal):
        m, l, acc = carry
        k0 = pl.multiple_of(j * tq, tq)
        s = lax.dot_general(q, k_ref[pl.ds(k0, tq), :], (((1,), (1,)), ((), ())), preferred_element_type=F32)
        if diagonal:
            rc = lax.broadcasted_iota(jnp.int32, (tq, tq), 0) // CHUNK
            cc = lax.broadcasted_iota(jnp.int32, (tq, tq), 1) // CHUNK
            s = jnp.where(rc >= cc, s, NEG_BIG)
        m_new = jnp.maximum(m, jnp.max(s, axis=-1, keepdims=True))
        p = jnp.exp(s - m_new)
        alpha = jnp.exp(m - m_new)
        l = alpha * l + jnp.sum(p, axis=-1, keepdims=True)
        acc = alpha * acc + jnp.dot(p.astype(BF16), v_ref[pl.ds(k0, tq), :], preferred_element_type=F32)
        return m_new, l, acc

    init = (jnp.full((tq, 1), NEG_BIG, F32), jnp.zeros((tq, 1), F32), jnp.zeros((tq, V_DIM), F32))
    carry = lax.fori_loop(0, i, lambda j, c: step(j, c, False), init)
    _, l, acc = step(i, carry, True)
    o_ref[...] = (acc / l).astype(o_ref.dtype)


def attention(q, k, v, batch, seq, heads, tq=512):
    tq = _tiles(seq, tq)
    q3 = q.reshape(batch, seq, heads * HEAD_PAD)
    k3 = k.reshape(batch, seq, heads * HEAD_PAD)
    v3 = v.reshape(batch, seq, heads * V_DIM)
    out = pl.pallas_call(
        functools.partial(_attn_kernel, tq=tq),
        grid=(batch, heads, seq // tq),
        in_specs=[pl.BlockSpec((None, tq, HEAD_PAD), lambda b, h, i: (b, i, h)),
                  pl.BlockSpec((None, seq, HEAD_PAD), lambda b, h, i: (b, 0, h)),
                  pl.BlockSpec((None, seq, V_DIM), lambda b, h, i: (b, 0, h))],
        out_specs=pl.BlockSpec((None, tq, V_DIM), lambda b, h, i: (b, i, h)),
        out_shape=jax.ShapeDtypeStruct((batch, seq, heads * V_DIM), BF16),
        compiler_params=_params("parallel", "parallel", "arbitrary"),
        name="attention",
    )(q3, k3, v3)
    return out.reshape(batch * seq, heads * V_DIM)


def _gated_out_kernel(a_ref, w_ref, g_ref, yin_ref, o_ref):
    out = jnp.dot(a_ref[...], w_ref[...], preferred_element_type=F32)
    o_ref[...] = (yin_ref[...].astype(F32) + _sigmoid(g_ref[...].astype(F32)) * out).astype(o_ref.dtype)


def gated_out(a, w, gates, g_blk, y_in, tm=512):
    t, k = a.shape
    d = w.shape[1]
    tm = _tiles(t, tm)
    return pl.pallas_call(
        _gated_out_kernel,
        grid=(t // tm,),
        in_specs=[pl.BlockSpec((tm, k), lambda i: (i, 0)), _resident((k, d)),
                  pl.BlockSpec((tm, d), lambda i: (i, g_blk)), pl.BlockSpec((tm, d), lambda i: (i, 0))],
        out_specs=pl.BlockSpec((tm, d), lambda i: (i, 0)),
        out_shape=jax.ShapeDtypeStruct((t, d), BF16),
        compiler_params=_params("parallel"),
        name="gated_out",
    )(a, w, gates, y_in)


def _router_kernel(x_ref, g_ref, wr_ref, br_ref, h_ref, w_ref, id_ref, *, n_groups, per_group):
    h = _rms(x_ref[...].astype(F32), g_ref[...])
    h_ref[...] = h.astype(h_ref.dtype)
    logits = jnp.dot(h, wr_ref[...], precision=lax.Precision.HIGHEST, preferred_element_type=F32) + br_ref[...]
    n_exp = n_groups * per_group
    lane = lax.broadcasted_iota(jnp.int32, logits.shape, 1)
    lane_f = lane.astype(F32)
    far = float(LANES)

    def softmax_where(mask):
        z = jnp.where(mask, logits, -jnp.inf)
        e = jnp.exp(z - jnp.max(z, axis=-1, keepdims=True))
        return jnp.where(mask, e / jnp.sum(e, axis=-1, keepdims=True), -1.0)

    def top1(p):
        best = jnp.max(p, axis=-1, keepdims=True)
        idx = jnp.min(jnp.where(p == best, lane_f, far), axis=-1, keepdims=True)
        return best, idx

    g_prob = softmax_where((lane >= n_exp) & (lane < n_exp + n_groups))
    g_w, g_lane = top1(g_prob)
    lo = (g_lane - float(n_exp)) * float(per_group)
    e_prob = softmax_where((lane_f >= lo) & (lane_f < lo + float(per_group)))
    p1, i1 = top1(e_prob)
    p2, i2 = top1(jnp.where(lane_f == i1, -1.0, e_prob))
    den = p1 + p2
    w_ref[...] = jnp.where(lane == 0, g_w * (p1 / den), jnp.where(lane == 1, g_w * (p2 / den), 0.0))
    id_ref[...] = jnp.where(lane == 0, i1, jnp.where(lane == 1, i2, 0.0)).astype(jnp.int32)


def router(x, ffn_norm, w_rg, b_rg, w_re, b_re, tm=256):
    t, d = x.shape
    n_groups, n_exp = w_rg.shape[1], w_re.shape[1]
    tm = _tiles(t, tm)
    pad = jnp.zeros((d, LANES - n_exp - n_groups), F32)
    wr = jnp.concatenate([w_re.astype(F32), w_rg.astype(F32), pad], axis=1)
    br = jnp.concatenate([b_re.astype(F32), b_rg.astype(F32), pad[0]]).reshape(1, LANES)
    h, w, ids = pl.pallas_call(
        functools.partial(_router_kernel, n_groups=n_groups, per_group=n_exp // n_groups),
        grid=(t // tm,),
        in_specs=[pl.BlockSpec((tm, d), lambda i: (i, 0)), _resident((1, d)), _resident((d, LANES)),
                  _resident((1, LANES))],
        out_specs=[pl.BlockSpec((tm, d), lambda i: (i, 0)), pl.BlockSpec((tm, LANES), lambda i: (i, 0)),
                   pl.BlockSpec((tm, LANES), lambda i: (i, 0))],
        out_shape=[jax.ShapeDtypeStruct((t, d), BF16), jax.ShapeDtypeStruct((t, LANES), F32),
                   jax.ShapeDtypeStruct((t, LANES), jnp.int32)],
        compiler_params=_params("parallel"),
        name="router",
    )(x, ffn_norm.reshape(1, d), wr, br)
    return h, w[:, :TOP_K], ids[:, :TOP_K]


def _expert_kernel(te_ref, na_ref, x_ref, wg_ref, wu_ref, wd_ref, rw_ref, o_ref):
    del te_ref
    active = pl.program_id(0) < na_ref[0]

    @pl.when(active)
    def _():
        x = x_ref[...]
        g = jnp.dot(x, wg_ref[...], preferred_element_type=F32)
        u = jnp.dot(x, wu_ref[...], preferred_element_type=F32)
        hid = (g * _sigmoid(g) * u).astype(BF16)
        out = jnp.dot(hid, wd_ref[...], preferred_element_type=F32)
        o_ref[...] = (out * rw_ref[...]).astype(o_ref.dtype)

    @pl.when(jnp.logical_not(active))
    def _():
        o_ref[...] = jnp.zeros(o_ref.shape, o_ref.dtype)


def expert_ffn(xs, row_w, tile_expert, n_active, w_gate, w_up, w_down, tile):
    p, d = xs.shape
    f = w_gate.shape[-1]
    grid_spec = pltpu.PrefetchScalarGridSpec(
        num_scalar_prefetch=2,
        grid=(p // tile,),
        in_specs=[pl.BlockSpec((tile, d), lambda i, te, na: (i, 0)),
                  pl.BlockSpec((None, d, f), lambda i, te, na: (te[i], 0, 0)),
                  pl.BlockSpec((None, d, f), lambda i, te, na: (te[i], 0, 0)),
                  pl.BlockSpec((None, f, d), lambda i, te, na: (te[i], 0, 0)),
                  pl.BlockSpec((tile, 1), lambda i, te, na: (i, 0))],
        out_specs=pl.BlockSpec((tile, d), lambda i, te, na: (i, 0)),
    )
    return pl.pallas_call(
        _expert_kernel,
        grid_spec=grid_spec,
        out_shape=jax.ShapeDtypeStruct((p, d), BF16),
        compiler_params=_params("arbitrary"),
        name="expert_ffn",
    )(tile_expert, n_active, xs, w_gate, w_up, w_down, row_w)


def _combine_kernel(x_ref, a_ref, b_ref, o_ref):
    o_ref[...] = x_ref[...] + a_ref[...].astype(F32) + b_ref[...].astype(F32)


def combine(x, r0, r1, tm=512):
    t, d = x.shape
    tm = _tiles(t, tm)
    spec = pl.BlockSpec((tm, d), lambda i: (i, 0))
    return pl.pallas_call(
        _combine_kernel, grid=(t // tm,), in_specs=[spec, spec, spec], out_specs=spec,
        out_shape=jax.ShapeDtypeStruct((t, d), x.dtype), compiler_params=_params("parallel"), name="combine",
    )(x, r0, r1)


def _dispatch_plan(ids, wts, n_exp, tile, n_tiles):
    n_assign = ids.size
    e_flat = ids.reshape(-1)
    order = jnp.argsort(e_flat, stable=True).astype(jnp.int32)
    e_sorted = e_flat[order]
    counts = jnp.bincount(e_flat, length=n_exp).astype(jnp.int32)
    starts = jnp.cumsum(counts) - counts
    padded = (counts + tile - 1) // tile * tile
    pad_ends = jnp.cumsum(padded)
    pad_starts = pad_ends - padded
    dest = pad_starts[e_sorted] + jnp.arange(n_assign, dtype=jnp.int32) - starts[e_sorted]
    rows = n_tiles * tile
    src_tok = jnp.zeros((rows,), jnp.int32).at[dest].set(order // TOP_K)
    row_w = jnp.zeros((rows,), F32).at[dest].set(wts.reshape(-1)[order])
    pos = jnp.zeros((n_assign,), jnp.int32).at[order].set(dest)
    n_active = (pad_ends[-1] // tile).astype(jnp.int32)
    tile_idx = jnp.minimum(jnp.arange(n_tiles, dtype=jnp.int32), n_active - 1)
    tile_expert = jnp.searchsorted(pad_ends, tile_idx * tile, side="right").astype(jnp.int32)
    tile_expert = jnp.minimum(tile_expert, n_exp - 1)
    return src_tok, row_w.reshape(rows, 1), pos.reshape(-1, TOP_K), tile_expert, n_active.reshape(1)


def hier_moe(x, ffn_norm, w_rg, b_rg, w_re, b_re, w_gate, w_up, w_down, tile=256):
    t = x.shape[0]
    n_exp = w_gate.shape[0]
    tile = _tiles(t, tile)
    h, wts, ids = router(x, ffn_norm, w_rg, b_rg, w_re, b_re)
    n_tiles = -(-(t * TOP_K + n_exp * (tile - 1)) // tile)
    src_tok, row_w, pos, tile_expert, n_active = _dispatch_plan(ids, wts, n_exp, tile, n_tiles)
    xs = jnp.take(h, src_tok, axis=0)
    rows = expert_ffn(xs, row_w, tile_expert, n_active, w_gate, w_up, w_down, tile)
    return combine(x, jnp.take(rows, pos[:, 0], axis=0), jnp.take(rows, pos[:, 1], axis=0))


def _rope_tables(positions):
    half = QK_ROPE // 2
    inv = jnp.power(ROPE_BASE, -jnp.arange(0, QK_ROPE, 2, dtype=F32) / QK_ROPE)
    ang = positions.astype(F32)[..., None] * inv
    cos = jnp.cos(ang).reshape(-1, half)
    sin = jnp.sin(ang).reshape(-1, half)
    z = jnp.zeros_like(cos)
    zz = jnp.zeros((cos.shape[0], LANES - QK_ROPE), F32)
    return (jnp.concatenate([cos, cos, zz], axis=1), jnp.concatenate([-sin, z, zz], axis=1),
            jnp.concatenate([z, sin, zz], axis=1))


def mixer_block(x, tabs, batch, seq, mix_norm, w_in, conv_w, conv_b, conv_norm, w_a_out, sg_norm, sg_w, sg_b,
                w_b_out, q_norm, w_uq, kv_norm, w_ukv, q_gain, k_gain, w_c_out, w_out):
    d = x.shape[1]
    c_conv = conv_w.shape[1]
    c_sg = sg_norm.shape[0]
    q_lora, kv_lora = q_norm.shape[0], kv_norm.shape[0]
    heads = w_uq.shape[1] // QK_DIM
    i0 = 2 * c_conv
    i1 = i0 + 2 * c_sg
    i2 = i1 + q_lora
    i3 = i2 + kv_lora
    i4 = i3 + QK_ROPE
    assert c_conv == c_sg and i1 % q_lora == 0 and i2 % kv_lora == 0, "column-block layout of the input projection"

    h = rmsnorm_rows(x, mix_norm)
    w_main = w_in[:, :i3].astype(BF16)
    w_kr = jnp.concatenate([w_in[:, i3:i4], jnp.zeros((d, LANES - QK_ROPE), w_in.dtype)], axis=1).astype(BF16)
    w_gates = w_in[:, i4:].astype(BF16)
    z_main = matmul(h, w_main, BF16, tn=512, name="in_proj_main")
    k_rope = matmul(h, w_kr, BF16, name="in_proj_rope")
    gates = matmul(h, w_gates, BF16, name="in_proj_gates")

    y = mixer_a(z_main, gates, conv_w, conv_b, conv_norm, w_a_out.astype(BF16), batch, seq, 0, 1, 0)
    y = mixer_b(z_main, gates, y, sg_norm, sg_w, sg_b, w_b_out.astype(BF16), 2, 3, 1)

    w_uq_pad = jnp.pad(w_uq.reshape(q_lora, heads, QK_DIM), ((0, 0), (0, 0), (0, HEAD_PAD - QK_DIM)))
    w_uq_pad = w_uq_pad.reshape(q_lora, heads * HEAD_PAD).astype(BF16)
    q = q_proj(z_main, i1 // q_lora, q_norm, w_uq_pad, q_gain, tabs, heads)
    k, v = kv_proj(z_main, i2 // kv_lora, k_rope, kv_norm, w_ukv.astype(BF16), k_gain, tabs, heads)
    o = attention(q, k, v, batch, seq, heads)
    y = gated_out(o, w_c_out.astype(BF16), gates, 2, y)

    return matmul(y, w_out.astype(BF16), F32, residual=x, name="out_proj")


def kernel(x, positions, mix_norm, w_in, conv_w, conv_b, conv_norm, w_a_out, sg_norm, sg_w, sg_b, w_b_out, q_norm, w_uq, kv_norm, w_ukv, q_gain, k_gain, w_c_out, w_out, ffn_norm, w_router_group, b_router_group, w_router_expert, b_router_expert, w_gate, w_up, w_down):
    batch, seq, d = x.shape
    tabs = _rope_tables(positions)
    xt = x.reshape(batch * seq, d)
    for l in range(mix_norm.shape[0]):
        xt = mixer_block(xt, tabs, batch, seq, mix_norm[l], w_in[l], conv_w[l], conv_b[l], conv_norm[l], w_a_out[l],
                         sg_norm[l], sg_w[l], sg_b[l], w_b_out[l], q_norm[l], w_uq[l], kv_norm[l], w_ukv[l],
                         q_gain[l], k_gain[l], w_c_out[l], w_out[l])
        xt = hier_moe(xt, ffn_norm[l], w_router_group[l], b_router_group[l], w_router_expert[l],
                      b_router_expert[l], w_gate[l].astype(BF16), w_up[l].astype(BF16), w_down[l].astype(BF16))
    return xt.reshape(batch, seq, d)
```

```python
import functools

import jax
import jax.numpy as jnp
from jax import lax
from jax.experimental import pallas as pl
from jax.experimental.pallas import tpu as pltpu

CHUNK = 64
EPS = 1e-6
QK_NOPE = 128
QK_ROPE = 64
V_DIM = 128
QK_DIM = QK_NOPE + QK_ROPE
ROPE_BASE = 10000.0
TOP_K = 2

LANES = 128
SUBLANES = 8
BF16_SUBLANES = 16
HEAD_PAD = 2 * LANES
VMEM_LIMIT_BYTES = 56 * 1024 * 1024
CONV_HIST = 32
CAST_BLOCK_BYTES = 4 * 1024 * 1024
NEG_BIG = -1e30
LOG2_E = 1.4426950408889634

F32 = jnp.float32
BF16 = jnp.bfloat16


def _tiles(n, pref):
    if n <= pref:
        return n
    for t in range(pref - pref % BF16_SUBLANES, 0, -BF16_SUBLANES):
        if n % t == 0:
            return t
    return n


def _params(*sem):
    return pltpu.CompilerParams(dimension_semantics=sem, vmem_limit_bytes=VMEM_LIMIT_BYTES)


def _resident(shape):
    nd = len(shape)
    return pl.BlockSpec(shape, lambda *_: (0,) * nd, pipeline_mode=pl.Buffered(1))


def _rms(xf, g):
    ms = jnp.mean(xf * xf, axis=-1, keepdims=True)
    return xf * lax.rsqrt(ms + EPS) * g


def _sigmoid(x):
    return 1.0 / (1.0 + jnp.exp(-x))


def _rmsnorm_kernel(x_ref, g_ref, o_ref):
    o_ref[...] = _rms(x_ref[...].astype(F32), g_ref[...]).astype(o_ref.dtype)


def rmsnorm_rows(x, g, out_dtype=BF16, tm=512):
    t, d = x.shape
    tm = _tiles(t, tm)
    return pl.pallas_call(
        _rmsnorm_kernel,
        grid=(t // tm,),
        in_specs=[pl.BlockSpec((tm, d), lambda i: (i, 0)), _resident((1, d))],
        out_specs=pl.BlockSpec((tm, d), lambda i: (i, 0)),
        out_shape=jax.ShapeDtypeStruct((t, d), out_dtype),
        compiler_params=_params("parallel"),
        name="rmsnorm_rows",
    )(x, g.reshape(1, d))


def _matmul_kernel(a_ref, b_ref, o_ref):
    o_ref[...] = jnp.dot(a_ref[...], b_ref[...], preferred_element_type=F32).astype(o_ref.dtype)


def _matmul_res_kernel(a_ref, b_ref, r_ref, o_ref):
    acc = jnp.dot(a_ref[...], b_ref[...], preferred_element_type=F32)
    o_ref[...] = (r_ref[...].astype(F32) + acc).astype(o_ref.dtype)


def matmul(a, b, out_dtype, residual=None, tm=1024, tn=1024, name="matmul"):
    m, k = a.shape
    n = b.shape[1]
    tm, tn = _tiles(m, tm), _tiles(n, tn)
    in_specs = [pl.BlockSpec((tm, k), lambda i, j: (i, 0)), pl.BlockSpec((k, tn), lambda i, j: (0, j))]
    args = [a, b]
    kern = _matmul_kernel
    if residual is not None:
        in_specs.append(pl.BlockSpec((tm, tn), lambda i, j: (i, j)))
        args.append(residual)
        kern = _matmul_res_kernel
    return pl.pallas_call(
        kern,
        grid=(m // tm, n // tn),
        in_specs=in_specs,
        out_specs=pl.BlockSpec((tm, tn), lambda i, j: (i, j)),
        out_shape=jax.ShapeDtypeStruct((m, n), out_dtype),
        compiler_params=_params("parallel", "arbitrary"),
        name=name,
    )(*args)


def _mixer_a_kernel(lin_ref, gate_ref, cw_ref, cb_ref, cn_ref, wout_ref, g_ref, o_ref, sh_ref, conv_ref,
                    *, ts, width, rows_per_step):
    c = lin_ref.shape[-1]
    off = CONV_HIST - (width - 1)
    n_shift = sh_ref.shape[1] - SUBLANES

    @pl.when(pl.program_id(1) == 0)
    def _():
        sh_ref[0, 0:CONV_HIST, :] = jnp.zeros((CONV_HIST, c), F32)

    sh_ref[0, CONV_HIST:CONV_HIST + ts, :] = lin_ref[...].astype(F32) * _sigmoid(gate_ref[...].astype(F32))
    for sft in range(1, SUBLANES):
        sh_ref[sft, 0:n_shift, :] = sh_ref[0, sft:sft + n_shift, :]

    for r0 in range(0, ts, rows_per_step):
        acc = jnp.broadcast_to(cb_ref[...], (rows_per_step, c))
        for k in range(width):
            sft, base = (off + k) % SUBLANES, r0 + (off + k) // SUBLANES * SUBLANES
            w_k = jnp.tile(cw_ref[k], (rows_per_step // SUBLANES, 1))
            acc = acc + w_k * sh_ref[sft, base:base + rows_per_step, :]
        conv_ref[r0:r0 + rows_per_step, :] = acc

    sh_ref[0, 0:CONV_HIST, :] = sh_ref[0, ts:ts + CONV_HIST, :]

    y = _rms(conv_ref[...], cn_ref[...])
    y = y * _sigmoid(y)
    out = jnp.dot(y.astype(BF16), wout_ref[...], preferred_element_type=F32)
    o_ref[...] = (_sigmoid(g_ref[...].astype(F32)) * out).astype(o_ref.dtype)


def mixer_a(z_main, gates, conv_w, conv_b, conv_norm, w_a_out, batch, seq, lin_blk, gate_blk, g_blk, ts=256):
    width, c = conv_w.shape
    d = w_a_out.shape[1]
    ts = _tiles(seq, ts)
    nt = seq // ts
    rows_per_step = min(16, ts)
    assert rows_per_step % SUBLANES == 0
    kern = functools.partial(_mixer_a_kernel, ts=ts, width=width, rows_per_step=rows_per_step)
    row = lambda b, t: b * nt + t
    return pl.pallas_call(
        kern,
        grid=(batch, nt),
        in_specs=[
            pl.BlockSpec((ts, c), lambda b, t: (row(b, t), lin_blk)),
            pl.BlockSpec((ts, c), lambda b, t: (row(b, t), gate_blk)),
            _resident((width, SUBLANES, c)), _resident((1, c)), _resident((1, c)),
            _resident((c, d)),
            pl.BlockSpec((ts, d), lambda b, t: (row(b, t), g_blk)),
        ],
        out_specs=pl.BlockSpec((ts, d), lambda b, t: (row(b, t), 0)),
        out_shape=jax.ShapeDtypeStruct((batch * seq, d), BF16),
        scratch_shapes=[pltpu.VMEM((SUBLANES, ts + CONV_HIST, c), F32), pltpu.VMEM((ts, c), F32)],
        compiler_params=_params("arbitrary", "arbitrary"),
        name="mixer_a",
    )(z_main, z_main, jnp.broadcast_to(conv_w[:, None, :], (width, SUBLANES, c)), conv_b.reshape(1, c),
      conv_norm.reshape(1, c), w_a_out, gates)


def _gelu_tanh(x):
    return 0.5 * x * (1.0 + jnp.tanh(0.7978845608028654 * (x + 0.044715 * (x * x * x))))


def _mixer_b_kernel(u_ref, v_ref, sn_ref, sw_ref, sb_ref, wout_ref, g_ref, yin_ref, o_ref, uv_ref,
                    *, n_blocks, groups, block):
    gd = u_ref.shape[-1] // groups
    v = _rms(_gelu_tanh(v_ref[...].astype(F32)), sn_ref[...]).astype(BF16)
    rc = lax.broadcasted_iota(jnp.int32, (block, block), 0) // CHUNK
    cc = lax.broadcasted_iota(jnp.int32, (block, block), 1) // CHUNK
    causal = rc >= cc
    for g in range(groups):
        w = jnp.where(causal, sw_ref[g], 0.0).astype(BF16)
        bias = sb_ref[:, g:g + 1]
        cols = slice(g * gd, (g + 1) * gd)
        for r in range(n_blocks):
            rows = slice(r * block, (r + 1) * block)
            sv = jnp.dot(w, v[rows, cols], preferred_element_type=F32) + bias
            u = _gelu_tanh(u_ref[rows, cols].astype(F32))
            uv_ref[rows, cols] = (u * sv).astype(BF16)
    out = jnp.dot(uv_ref[...], wout_ref[...], preferred_element_type=F32)
    o_ref[...] = (yin_ref[...].astype(F32) + _sigmoid(g_ref[...].astype(F32)) * out).astype(o_ref.dtype)


def mixer_b(z_main, gates, y_in, sg_norm, sg_w, sg_b, w_b_out, u_blk, v_blk, g_blk, ts=512):
    groups, block, _ = sg_w.shape
    c, d = w_b_out.shape
    t = z_main.shape[0]
    ts = max(_tiles(t, ts), block)
    kern = functools.partial(_mixer_b_kernel, n_blocks=ts // block, groups=groups, block=block)
    return pl.pallas_call(
        kern,
        grid=(t // ts,),
        in_specs=[
            pl.BlockSpec((ts, c), lambda i: (i, u_blk)),
            pl.BlockSpec((ts, c), lambda i: (i, v_blk)),
            _resident((1, c)), _resident((groups, block, block)), _resident((block, groups)),
            _resident((c, d)),
            pl.BlockSpec((ts, d), lambda i: (i, g_blk)),
            pl.BlockSpec((ts, d), lambda i: (i, 0)),
        ],
        out_specs=pl.BlockSpec((ts, d), lambda i: (i, 0)),
        out_shape=jax.ShapeDtypeStruct((t, d), BF16),
        scratch_shapes=[pltpu.VMEM((ts, c), BF16)],
        compiler_params=_params("parallel"),
        name="mixer_b",
    )(z_main, z_main, sg_norm.reshape(1, c), sg_w, jnp.transpose(sg_b), w_b_out, gates, y_in)


def _rope_group(x, cos_ref, sa_ref, sb_ref):
    half = QK_ROPE // 2
    return (x * cos_ref[...] + pltpu.roll(x, LANES - half, 1) * sa_ref[...]
            + pltpu.roll(x, half, 1) * sb_ref[...])


def _norm_group(x, gain, n):
    ms = jnp.sum(x * x, axis=-1, keepdims=True) * (1.0 / n)
    return x * lax.rsqrt(ms + EPS) * gain


def _q_proj_kernel(cq_ref, qn_ref, wq_ref, gn_ref, gr_ref, cos_ref, sa_ref, sb_ref, o_ref, *, heads):
    h = _rms(cq_ref[...].astype(F32), qn_ref[...]).astype(BF16)
    q = jnp.dot(h, wq_ref[...], preferred_element_type=F32)
    scale = QK_DIM ** -0.5 * LOG2_E
    for hd in range(heads):
        c0 = hd * HEAD_PAD
        nope = _norm_group(q[:, c0:c0 + QK_NOPE], gn_ref[...], QK_NOPE)
        rope = _norm_group(q[:, c0 + QK_NOPE:c0 + HEAD_PAD], gr_ref[...], QK_ROPE)
        rope = _rope_group(rope, cos_ref, sa_ref, sb_ref)
        o_ref[:, c0:c0 + QK_NOPE] = (nope * scale).astype(o_ref.dtype)
        o_ref[:, c0 + QK_NOPE:c0 + HEAD_PAD] = (rope * scale).astype(o_ref.dtype)


def _kv_proj_kernel(ckv_ref, kr_ref, kvn_ref, wkv_ref, gn_ref, gr_ref, cos_ref, sa_ref, sb_ref,
                    k_ref, v_ref, *, heads):
    h = _rms(ckv_ref[...].astype(F32), kvn_ref[...]).astype(BF16)
    kv = jnp.dot(h, wkv_ref[...], preferred_element_type=F32)
    rope = _norm_group(kr_ref[...].astype(F32), gr_ref[...], QK_ROPE)
    rope = _rope_group(rope, cos_ref, sa_ref, sb_ref).astype(k_ref.dtype)
    for hd in range(heads):
        c0 = hd * (QK_NOPE + V_DIM)
        nope = _norm_group(kv[:, c0:c0 + QK_NOPE], gn_ref[...], QK_NOPE)
        k_ref[:, hd * HEAD_PAD:hd * HEAD_PAD + QK_NOPE] = nope.astype(k_ref.dtype)
        k_ref[:, hd * HEAD_PAD + QK_NOPE:(hd + 1) * HEAD_PAD] = rope
        v_ref[:, 2 * hd * V_DIM:(2 * hd + 1) * V_DIM] = kv[:, c0 + QK_NOPE:c0 + QK_NOPE + V_DIM].astype(v_ref.dtype)
        v_ref[:, (2 * hd + 1) * V_DIM:(2 * hd + 2) * V_DIM] = jnp.ones((kv.shape[0], V_DIM), v_ref.dtype)


def _pad_gain(gain):
    gn = gain[:QK_NOPE].reshape(1, QK_NOPE)
    gr = jnp.concatenate([gain[QK_NOPE:], jnp.zeros((LANES - QK_ROPE,), gain.dtype)]).reshape(1, LANES)
    return gn, gr


def q_proj(z_main, cq_blk, q_norm, w_uq_pad, q_gain, tabs, heads, tm=256):
    t = z_main.shape[0]
    ql = q_norm.shape[0]
    tm = _tiles(t, tm)
    gn, gr = _pad_gain(q_gain)
    tab_spec = pl.BlockSpec((tm, LANES), lambda i: (i, 0))
    return pl.pallas_call(
        functools.partial(_q_proj_kernel, heads=heads),
        grid=(t // tm,),
        in_specs=[pl.BlockSpec((tm, ql), lambda i: (i, cq_blk)), _resident((1, ql)),
                  _resident((ql, heads * HEAD_PAD)), _resident((1, QK_NOPE)), _resident((1, LANES)),
                  tab_spec, tab_spec, tab_spec],
        out_specs=pl.BlockSpec((tm, heads * HEAD_PAD), lambda i: (i, 0)),
        out_shape=jax.ShapeDtypeStruct((t, heads * HEAD_PAD), BF16),
        compiler_params=_params("parallel"),
        name="q_proj",
    )(z_main, q_norm.reshape(1, ql), w_uq_pad, gn, gr, *tabs)


def kv_proj(z_main, ckv_blk, k_rope, kv_norm, w_ukv, k_gain, tabs, heads, tm=256):
    t = z_main.shape[0]
    kl = kv_norm.shape[0]
    tm = _tiles(t, tm)
    gn, gr = _pad_gain(k_gain)
    tab_spec = pl.BlockSpec((tm, LANES), lambda i: (i, 0))
    return pl.pallas_call(
        functools.partial(_kv_proj_kernel, heads=heads),
        grid=(t // tm,),
        in_specs=[pl.BlockSpec((tm, kl), lambda i: (i, ckv_blk)), tab_spec, _resident((1, kl)),
                  _resident((kl, heads * (QK_NOPE + V_DIM))), _resident((1, QK_NOPE)), _resident((1, LANES)),
                  tab_spec, tab_spec, tab_spec],
        out_specs=[pl.BlockSpec((tm, heads * HEAD_PAD), lambda i: (i, 0)),
                   pl.BlockSpec((tm, heads * 2 * V_DIM), lambda i: (i, 0))],
        out_shape=[jax.ShapeDtypeStruct((t, heads * HEAD_PAD), BF16),
                   jax.ShapeDtypeStruct((t, heads * 2 * V_DIM), BF16)],
        compiler_params=_params("parallel"),
        name="kv_proj",
    )(z_main, k_rope, kv_norm.reshape(1, kl), w_ukv, gn, gr, *tabs)


def _attn_kernel(q_ref, k_ref, v_ref, o_ref, m_ref, acc_ref, p_ref, a_ref, *, tq, tk, heads_per_step):
    i = pl.program_id(2)
    wv = 2 * V_DIM
    per_q = tq // tk
    assert per_q == 2, "the slot ping-pong below is written for two key tiles per query tile"

    m_ref[...] = jnp.full(m_ref.shape, NEG_BIG, F32)
    acc_ref[...] = jnp.zeros(acc_ref.shape, F32)
    p_ref[1] = jnp.zeros(p_ref.shape[1:], BF16)
    a_ref[1] = jnp.ones(a_ref.shape[1:], F32)

    def scores(j, masked):
        k0 = pl.multiple_of(j * tk, tk)
        out = []
        for hd in range(heads_per_step):
            q = q_ref[:, hd * HEAD_PAD:(hd + 1) * HEAD_PAD]
            k = k_ref[pl.ds(k0, tk), hd * HEAD_PAD:(hd + 1) * HEAD_PAD]
            s = lax.dot_general(q, k, (((1,), (1,)), ((), ())), preferred_element_type=F32)
            if masked:
                q_chunk = (i * tq + lax.broadcasted_iota(jnp.int32, (tq, tk), 0)) // CHUNK
                k_chunk = (k0 + lax.broadcasted_iota(jnp.int32, (tq, tk), 1)) // CHUNK
                s = jnp.where(q_chunk >= k_chunk, s, NEG_BIG)
            out.append(s)
        return out

    def probs(s_heads, slot):
        for hd, s in enumerate(s_heads):
            m_prev = m_ref[hd]
            m_new = jnp.maximum(m_prev, jnp.max(s, axis=-1, keepdims=True))
            p_ref[slot, hd] = jnp.exp2(s - jnp.tile(m_new, (1, tk // LANES))).astype(BF16)
            a_ref[slot, hd] = jnp.exp2(m_prev - m_new)
            m_ref[hd] = m_new

    def accumulate(j, slot):
        k0 = pl.multiple_of(jnp.maximum(j, 0) * tk, tk)
        for hd in range(heads_per_step):
            pv = jnp.dot(p_ref[slot, hd], v_ref[pl.ds(k0, tk), hd * wv:(hd + 1) * wv], preferred_element_type=F32)
            acc_ref[hd] = jnp.tile(a_ref[slot, hd], (1, wv // LANES)) * acc_ref[hd] + pv

    def tile_pair(j, masked):
        accumulate(j - 1, 1)
        s_even = scores(j, masked)
        s_odd = scores(j + 1, masked)
        probs(s_even, 0)
        accumulate(j, 0)
        probs(s_odd, 1)

    def body(t, carry):
        tile_pair(2 * t, False)
        return carry

    lax.fori_loop(0, i, body, 0)
    tile_pair(2 * i, True)
    accumulate(2 * i + 1, 1)
    for hd in range(heads_per_step):
        acc = acc_ref[hd]
        o_ref[:, hd * V_DIM:(hd + 1) * V_DIM] = (acc[:, :V_DIM] / acc[:, V_DIM:V_DIM + 1]).astype(o_ref.dtype)


def attention(q, k, v, batch, seq, heads, tq=1024, heads_per_step=2):
    tq = _tiles(seq, tq)
    tk = tq // 2
    hps = heads_per_step if heads % heads_per_step == 0 else 1
    q3 = q.reshape(batch, seq, heads * HEAD_PAD)
    k3 = k.reshape(batch, seq, heads * HEAD_PAD)
    v3 = v.reshape(batch, seq, heads * 2 * V_DIM)
    out = pl.pallas_call(
        functools.partial(_attn_kernel, tq=tq, tk=tk, heads_per_step=hps),
        grid=(batch, heads // hps, seq // tq),
        in_specs=[pl.BlockSpec((None, tq, hps * HEAD_PAD), lambda b, h, i: (b, i, h)),
                  pl.BlockSpec((None, seq, hps * HEAD_PAD), lambda b, h, i: (b, 0, h)),
                  pl.BlockSpec((None, seq, hps * 2 * V_DIM), lambda b, h, i: (b, 0, h))],
        out_specs=pl.BlockSpec((None, tq, hps * V_DIM), lambda b, h, i: (b, i, h)),
        out_shape=jax.ShapeDtypeStruct((batch, seq, heads * V_DIM), BF16),
        scratch_shapes=[pltpu.VMEM((hps, tq, LANES), F32), pltpu.VMEM((hps, tq, 2 * V_DIM), F32),
                        pltpu.VMEM((2, hps, tq, tk), BF16), pltpu.VMEM((2, hps, tq, LANES), F32)],
        compiler_params=_params("parallel", "parallel", "arbitrary"),
        name="attention",
    )(q3, k3, v3)
    return out.reshape(batch * seq, heads * V_DIM)


def _gated_out_kernel(a_ref, w_ref, g_ref, yin_ref, o_ref):
    out = jnp.dot(a_ref[...], w_ref[...], preferred_element_type=F32)
    o_ref[...] = (yin_ref[...].astype(F32) + _sigmoid(g_ref[...].astype(F32)) * out).astype(o_ref.dtype)


def gated_out(a, w, gates, g_blk, y_in, tm=512):
    t, k = a.shape
    d = w.shape[1]
    tm = _tiles(t, tm)
    return pl.pallas_call(
        _gated_out_kernel,
        grid=(t // tm,),
        in_specs=[pl.BlockSpec((tm, k), lambda i: (i, 0)), _resident((k, d)),
                  pl.BlockSpec((tm, d), lambda i: (i, g_blk)), pl.BlockSpec((tm, d), lambda i: (i, 0))],
        out_specs=pl.BlockSpec((tm, d), lambda i: (i, 0)),
        out_shape=jax.ShapeDtypeStruct((t, d), BF16),
        compiler_params=_params("parallel"),
        name="gated_out",
    )(a, w, gates, y_in)


def _router_kernel(x_ref, g_ref, wr_ref, br_ref, h_ref, w_ref, id_ref, *, n_groups, per_group):
    h = _rms(x_ref[...].astype(F32), g_ref[...])
    h_ref[...] = h.astype(h_ref.dtype)
    logits = jnp.dot(h, wr_ref[...], precision=lax.Precision.HIGHEST, preferred_element_type=F32) + br_ref[...]
    n_exp = n_groups * per_group
    lane = lax.broadcasted_iota(jnp.int32, logits.shape, 1)
    lane_f = lane.astype(F32)
    far = float(LANES)

    def softmax_where(mask):
        z = jnp.where(mask, logits, -jnp.inf)
        e = jnp.exp(z - jnp.max(z, axis=-1, keepdims=True))
        return jnp.where(mask, e / jnp.sum(e, axis=-1, keepdims=True), -1.0)

    def top1(p):
        best = jnp.max(p, axis=-1, keepdims=True)
        idx = jnp.min(jnp.where(p == best, lane_f, far), axis=-1, keepdims=True)
        return best, idx

    g_prob = softmax_where((lane >= n_exp) & (lane < n_exp + n_groups))
    g_w, g_lane = top1(g_prob)
    lo = (g_lane - float(n_exp)) * float(per_group)
    e_prob = softmax_where((lane_f >= lo) & (lane_f < lo + float(per_group)))
    p1, i1 = top1(e_prob)
    p2, i2 = top1(jnp.where(lane_f == i1, -1.0, e_prob))
    den = p1 + p2
    w_ref[...] = jnp.where(lane == 0, g_w * (p1 / den), jnp.where(lane == 1, g_w * (p2 / den), 0.0))
    id_ref[...] = jnp.where(lane == 0, i1, jnp.where(lane == 1, i2, 0.0)).astype(jnp.int32)


def router(x, ffn_norm, w_rg, b_rg, w_re, b_re, tm=256):
    t, d = x.shape
    n_groups, n_exp = w_rg.shape[1], w_re.shape[1]
    tm = _tiles(t, tm)
    pad = jnp.zeros((d, LANES - n_exp - n_groups), F32)
    wr = jnp.concatenate([w_re.astype(F32), w_rg.astype(F32), pad], axis=1)
    br = jnp.concatenate([b_re.astype(F32), b_rg.astype(F32), pad[0]]).reshape(1, LANES)
    h, w, ids = pl.pallas_call(
        functools.partial(_router_kernel, n_groups=n_groups, per_group=n_exp // n_groups),
        grid=(t // tm,),
        in_specs=[pl.BlockSpec((tm, d), lambda i: (i, 0)), _resident((1, d)), _resident((d, LANES)),
                  _resident((1, LANES))],
        out_specs=[pl.BlockSpec((tm, d), lambda i: (i, 0)), pl.BlockSpec((tm, LANES), lambda i: (i, 0)),
                   pl.BlockSpec((tm, LANES), lambda i: (i, 0))],
        out_shape=[jax.ShapeDtypeStruct((t, d), BF16), jax.ShapeDtypeStruct((t, LANES), F32),
                   jax.ShapeDtypeStruct((t, LANES), jnp.int32)],
        compiler_params=_params("parallel"),
        name="router",
    )(x, ffn_norm.reshape(1, d), wr, br)
    return h, w, ids


def _expert_kernel(te_ref, na_ref, x_ref, wg_ref, wu_ref, wd_ref, o_ref):
    del te_ref
    active = pl.program_id(0) < na_ref[0]

    @pl.when(active)
    def _():
        x = x_ref[...]
        g = jnp.dot(x, wg_ref[...], preferred_element_type=F32)
        u = jnp.dot(x, wu_ref[...], preferred_element_type=F32)
        hid = (g * _sigmoid(g) * u).astype(BF16)
        o_ref[...] = jnp.dot(hid, wd_ref[...], preferred_element_type=F32).astype(o_ref.dtype)

    @pl.when(jnp.logical_not(active))
    def _():
        o_ref[...] = jnp.zeros(o_ref.shape, o_ref.dtype)


def expert_ffn(xs, tile_expert, n_active, w_gate, w_up, w_down, tile):
    p, d = xs.shape
    f = w_gate.shape[-1]
    grid_spec = pltpu.PrefetchScalarGridSpec(
        num_scalar_prefetch=2,
        grid=(p // tile,),
        in_specs=[pl.BlockSpec((tile, d), lambda i, te, na: (i, 0)),
                  pl.BlockSpec((None, d, f), lambda i, te, na: (te[i], 0, 0)),
                  pl.BlockSpec((None, d, f), lambda i, te, na: (te[i], 0, 0)),
                  pl.BlockSpec((None, f, d), lambda i, te, na: (te[i], 0, 0))],
        out_specs=pl.BlockSpec((tile, d), lambda i, te, na: (i, 0)),
    )
    return pl.pallas_call(
        _expert_kernel,
        grid_spec=grid_spec,
        out_shape=jax.ShapeDtypeStruct((p, d), BF16),
        compiler_params=_params("arbitrary"),
        name="expert_ffn",
    )(tile_expert, n_active, xs, w_gate, w_up, w_down)


def _combine_kernel(x_ref, w_ref, a_ref, b_ref, o_ref):
    w = w_ref[...]
    o_ref[...] = x_ref[...] + w[:, 0:1] * a_ref[...].astype(F32) + w[:, 1:2] * b_ref[...].astype(F32)


def combine(x, w, r0, r1, tm=256):
    t, d = x.shape
    tm = _tiles(t, tm)
    spec = pl.BlockSpec((tm, d), lambda i: (i, 0))
    return pl.pallas_call(
        _combine_kernel, grid=(t // tm,),
        in_specs=[spec, pl.BlockSpec((tm, LANES), lambda i: (i, 0)), spec, spec], out_specs=spec,
        out_shape=jax.ShapeDtypeStruct((t, d), x.dtype), compiler_params=_params("parallel"), name="combine",
    )(x, w, r0, r1)


def _cast_kernel(x_ref, o_ref):
    o_ref[...] = x_ref[...].astype(o_ref.dtype)


def cast_layer(w, layer, out_dtype=BF16, cols=None):
    lead = w.shape[1:-2]
    a, b = w.shape[-2:]
    b = b if cols is None else cols
    ta = _tiles(a, max(8, CAST_BLOCK_BYTES // (b * w.dtype.itemsize)))
    n_lead = len(lead)
    grid = lead + (a // ta,)
    in_spec = pl.BlockSpec((None,) * (1 + n_lead) + (ta, b), lambda *ix: (layer,) + ix + (0,))
    out_spec = pl.BlockSpec((None,) * n_lead + (ta, b), lambda *ix: ix + (0,))
    return pl.pallas_call(
        _cast_kernel, grid=grid, in_specs=[in_spec], out_specs=out_spec,
        out_shape=jax.ShapeDtypeStruct(lead + (a, b), out_dtype),
        compiler_params=_params(*(("parallel",) * len(grid))), name="cast_layer",
    )(w)


def _dispatch_plan(ids, n_exp, tile, n_tiles):
    n_assign = ids.size
    e_flat = ids.reshape(-1)
    iota = jnp.arange(n_assign, dtype=jnp.int32)
    e_sorted, order = lax.sort_key_val(e_flat, iota)
    experts = jnp.arange(n_exp, dtype=jnp.int32)
    counts = jnp.sum((e_flat[:, None] == experts[None, :]).astype(jnp.int32), axis=0)
    starts = jnp.cumsum(counts) - counts
    padded = (counts + tile - 1) // tile * tile
    pad_ends = jnp.cumsum(padded)
    pad_starts = pad_ends - padded
    shift = pad_starts - starts
    dest_sorted = iota + jnp.sum(jnp.where(e_sorted[:, None] == experts[None, :], shift[None, :], 0), axis=1)
    _, pos = lax.sort_key_val(order, dest_sorted)
    n_active = pad_ends[-1] // tile
    tile_idx = jnp.minimum(jnp.arange(n_tiles, dtype=jnp.int32), n_active - 1)
    tile_expert = jnp.sum((pad_ends[None, :] <= (tile_idx * tile)[:, None]).astype(jnp.int32), axis=1)
    tile_expert = jnp.minimum(tile_expert, n_exp - 1)
    slot = jnp.arange(n_tiles * tile, dtype=jnp.int32)
    slot_expert = jnp.repeat(tile_expert, tile)
    rank = slot - pad_starts[slot_expert]
    valid = rank < counts[slot_expert]
    src_assign = order.at[jnp.clip(starts[slot_expert] + rank, 0, n_assign - 1)].get(mode="promise_in_bounds")
    src_tok = jnp.where(valid, src_assign // TOP_K, 0)
    return src_tok, pos.reshape(-1, TOP_K), tile_expert, n_active.reshape(1).astype(jnp.int32)


def _take_rows(x, idx):
    return x.at[idx].get(mode="promise_in_bounds")


def hier_moe(x, ffn_norm, w_rg, b_rg, w_re, b_re, w_gate, w_up, w_down, tile=256):
    t = x.shape[0]
    n_exp = w_gate.shape[0]
    tile = _tiles(t, tile)
    h, wts, ids = router(x, ffn_norm, w_rg, b_rg, w_re, b_re)
    n_tiles = -(-(t * TOP_K + n_exp * (tile - 1)) // tile)
    src_tok, pos, tile_expert, n_active = _dispatch_plan(ids[:, :TOP_K], n_exp, tile, n_tiles)
    rows = expert_ffn(_take_rows(h, src_tok), tile_expert, n_active, w_gate, w_up, w_down, tile)
    return combine(x, wts, _take_rows(rows, pos[:, 0]), _take_rows(rows, pos[:, 1]))


def _rope_tables(positions):
    half = QK_ROPE // 2
    inv = jnp.power(ROPE_BASE, -jnp.arange(0, QK_ROPE, 2, dtype=F32) / QK_ROPE)
    ang = positions.astype(F32)[..., None] * inv
    cos = jnp.cos(ang).reshape(-1, half)
    sin = jnp.sin(ang).reshape(-1, half)
    z = jnp.zeros_like(cos)
    zz = jnp.zeros((cos.shape[0], LANES - QK_ROPE), F32)
    return (jnp.concatenate([cos, cos, zz], axis=1), jnp.concatenate([-sin, z, zz], axis=1),
            jnp.concatenate([z, sin, zz], axis=1))


def mixer_block(x, tabs, batch, seq, layer, mix_norm, w_in_all, conv_w, conv_b, conv_norm, w_a_out_all, sg_norm,
                sg_w, sg_b, w_b_out_all, q_norm, w_uq, kv_norm, w_ukv_all, q_gain, k_gain, w_c_out_all, w_out_all):
    w_in = w_in_all[layer]
    d = x.shape[1]
    c_conv = conv_w.shape[1]
    c_sg = sg_norm.shape[0]
    q_lora, kv_lora = q_norm.shape[0], kv_norm.shape[0]
    heads = w_uq.shape[1] // QK_DIM
    i0 = 2 * c_conv
    i1 = i0 + 2 * c_sg
    i2 = i1 + q_lora
    i3 = i2 + kv_lora
    i4 = i3 + QK_ROPE
    assert c_conv == c_sg and i1 % q_lora == 0 and i2 % kv_lora == 0, "column-block layout of the input projection"

    h = rmsnorm_rows(x, mix_norm)
    w_main = cast_layer(w_in_all, layer, cols=i3)
    w_kr = jnp.concatenate([w_in[:, i3:i4], jnp.zeros((d, LANES - QK_ROPE), w_in.dtype)], axis=1).astype(BF16)
    w_gates = w_in[:, i4:].astype(BF16)
    z_main = matmul(h, w_main, BF16, tn=512, name="in_proj_main")
    k_rope = matmul(h, w_kr, BF16, name="in_proj_rope")
    gates = matmul(h, w_gates, BF16, name="in_proj_gates")

    y = mixer_a(z_main, gates, conv_w, conv_b, conv_norm, cast_layer(w_a_out_all, layer), batch, seq, 0, 1, 0)
    y = mixer_b(z_main, gates, y, sg_norm, sg_w, sg_b, cast_layer(w_b_out_all, layer), 2, 3, 1)

    w_uq_pad = jnp.pad(w_uq.reshape(q_lora, heads, QK_DIM), ((0, 0), (0, 0), (0, HEAD_PAD - QK_DIM)))
    w_uq_pad = w_uq_pad.reshape(q_lora, heads * HEAD_PAD).astype(BF16)
    q = q_proj(z_main, i1 // q_lora, q_norm, w_uq_pad, q_gain, tabs, heads)
    k, v = kv_proj(z_main, i2 // kv_lora, k_rope, kv_norm, cast_layer(w_ukv_all, layer), k_gain, tabs, heads)
    o = attention(q, k, v, batch, seq, heads)
    y = gated_out(o, cast_layer(w_c_out_all, layer), gates, 2, y)

    return matmul(y, cast_layer(w_out_all, layer), F32, residual=x, name="out_proj")


def kernel(x, positions, mix_norm, w_in, conv_w, conv_b, conv_norm, w_a_out, sg_norm, sg_w, sg_b, w_b_out, q_norm, w_uq, kv_norm, w_ukv, q_gain, k_gain, w_c_out, w_out, ffn_norm, w_router_group, b_router_group, w_router_expert, b_router_expert, w_gate, w_up, w_down):
    batch, seq, d = x.shape
    tabs = _rope_tables(positions)
    xt = x.reshape(batch * seq, d)
    for l in range(mix_norm.shape[0]):
        xt = mixer_block(xt, tabs, batch, seq, l, mix_norm[l], w_in, conv_w[l], conv_b[l], conv_norm[l], w_a_out,
                         sg_norm[l], sg_w[l], sg_b[l], w_b_out, q_norm[l], w_uq[l], kv_norm[l], w_ukv,
                         q_gain[l], k_gain[l], w_c_out, w_out)
        xt = hier_moe(xt, ffn_norm[l], w_router_group[l], b_router_group[l], w_router_expert[l],
                      b_router_expert[l], cast_layer(w_gate, l), cast_layer(w_up, l), cast_layer(w_down, l))
    return xt.reshape(batch, seq, d)
```

```python
import functools

import jax
import jax.numpy as jnp
from jax import lax
from jax.experimental import pallas as pl
from jax.experimental.pallas import tpu as pltpu

CHUNK = 64
EPS = 1e-6
QK_NOPE = 128
QK_ROPE = 64
V_DIM = 128
QK_DIM = QK_NOPE + QK_ROPE
ROPE_BASE = 10000.0
TOP_K = 2

LANES = 128
SUBLANES = 8
BF16_SUBLANES = 16
HEAD_PAD = 2 * LANES
VMEM_LIMIT_BYTES = 56 * 1024 * 1024
CONV_HIST = 32
CAST_CHUNK_BYTES = 1024 * 1024
CAST_DEPTH = 8
NEG_BIG = -1e30
LOG2_E = 1.4426950408889634

F32 = jnp.float32
BF16 = jnp.bfloat16


def _tiles(n, pref):
    if n <= pref:
        return n
    for t in range(pref - pref % BF16_SUBLANES, 0, -BF16_SUBLANES):
        if n % t == 0:
            return t
    return n


def _params(*sem):
    return pltpu.CompilerParams(dimension_semantics=sem, vmem_limit_bytes=VMEM_LIMIT_BYTES)


def _resident(shape):
    nd = len(shape)
    return pl.BlockSpec(shape, lambda *_: (0,) * nd, pipeline_mode=pl.Buffered(1))


def _rms(xf, g):
    ms = jnp.mean(xf * xf, axis=-1, keepdims=True)
    return xf * lax.rsqrt(ms + EPS) * g


def _sigmoid(x):
    return 1.0 / (1.0 + jnp.exp(-x))


def _rmsnorm_kernel(x_ref, g_ref, o_ref):
    o_ref[...] = _rms(x_ref[...].astype(F32), g_ref[...]).astype(o_ref.dtype)


def rmsnorm_rows(x, g, out_dtype=BF16, tm=512):
    t, d = x.shape
    tm = _tiles(t, tm)
    return pl.pallas_call(
        _rmsnorm_kernel,
        grid=(t // tm,),
        in_specs=[pl.BlockSpec((tm, d), lambda i: (i, 0)), _resident((1, d))],
        out_specs=pl.BlockSpec((tm, d), lambda i: (i, 0)),
        out_shape=jax.ShapeDtypeStruct((t, d), out_dtype),
        compiler_params=_params("parallel"),
        name="rmsnorm_rows",
    )(x, g.reshape(1, d))


def _matmul_kernel(a_ref, b_ref, o_ref):
    o_ref[...] = jnp.dot(a_ref[...], b_ref[...], preferred_element_type=F32).astype(o_ref.dtype)


def _matmul_res_kernel(a_ref, b_ref, r_ref, o_ref):
    acc = jnp.dot(a_ref[...], b_ref[...], preferred_element_type=F32)
    o_ref[...] = (r_ref[...].astype(F32) + acc).astype(o_ref.dtype)


def matmul(a, b, out_dtype, residual=None, tm=1024, tn=1024, name="matmul"):
    m, k = a.shape
    n = b.shape[1]
    tm, tn = _tiles(m, tm), _tiles(n, tn)
    in_specs = [pl.BlockSpec((tm, k), lambda i, j: (i, 0)), pl.BlockSpec((k, tn), lambda i, j: (0, j))]
    args = [a, b]
    kern = _matmul_kernel
    if residual is not None:
        in_specs.append(pl.BlockSpec((tm, tn), lambda i, j: (i, j)))
        args.append(residual)
        kern = _matmul_res_kernel
    return pl.pallas_call(
        kern,
        grid=(m // tm, n // tn),
        in_specs=in_specs,
        out_specs=pl.BlockSpec((tm, tn), lambda i, j: (i, j)),
        out_shape=jax.ShapeDtypeStruct((m, n), out_dtype),
        compiler_params=_params("parallel", "arbitrary"),
        name=name,
    )(*args)


def _matmul_wf32_kernel(*refs, shift, k_chunk, has_res):
    a_ref, w_ref = refs[0], refs[1]
    rest = list(refs[2:])
    wn_ref = rest.pop(0) if shift else None
    r_ref = rest.pop(0) if has_res else None
    o_ref, wb_ref = rest
    k = w_ref.shape[0]

    @pl.when(pl.program_id(1) == 0)
    def _():
        for k0 in range(0, k, k_chunk):
            w = w_ref[k0:k0 + k_chunk, :]
            if shift:
                w = jnp.concatenate([w[:, shift:], wn_ref[k0:k0 + k_chunk, :shift]], axis=1)
            wb_ref[k0:k0 + k_chunk, :] = w.astype(BF16)

    acc = jnp.dot(a_ref[...], wb_ref[...], preferred_element_type=F32)
    if has_res:
        acc = acc + r_ref[...].astype(F32)
    o_ref[...] = acc.astype(o_ref.dtype)


def matmul_wf32(a, w_all, layer, col0, n, out_dtype, shift=0, residual=None, tm=1024, tn=512, name="matmul_wf32"):
    m, k = a.shape
    tm = _tiles(m, tm)
    tn = max(t for t in range(LANES, max(tn, LANES) + 1, LANES) if n % t == 0 and col0 % t == 0)
    assert 0 <= shift < LANES
    c0 = col0 // tn
    in_specs = [pl.BlockSpec((tm, k), lambda j, i: (i, 0)),
                pl.BlockSpec((None, k, tn), lambda j, i: (layer, 0, c0 + j), pipeline_mode=pl.Buffered(1))]
    args = [a, w_all]
    if shift:
        in_specs.append(pl.BlockSpec((None, k, LANES), lambda j, i: (layer, 0, (c0 + j + 1) * (tn // LANES)),
                                     pipeline_mode=pl.Buffered(1)))
        args.append(w_all)
    if residual is not None:
        in_specs.append(pl.BlockSpec((tm, tn), lambda j, i: (i, j)))
        args.append(residual)
    kern = functools.partial(_matmul_wf32_kernel, shift=shift, k_chunk=_tiles(k, 512), has_res=residual is not None)
    return pl.pallas_call(
        kern,
        grid=(n // tn, m // tm),
        in_specs=in_specs,
        out_specs=pl.BlockSpec((tm, tn), lambda j, i: (i, j)),
        out_shape=jax.ShapeDtypeStruct((m, n), out_dtype),
        scratch_shapes=[pltpu.VMEM((k, tn), BF16)],
        compiler_params=_params("arbitrary", "arbitrary"),
        name=name,
    )(*args)


def _mixer_a_kernel(lin_ref, gate_ref, cw_ref, cb_ref, cn_ref, wout_ref, g_ref, o_ref, sh_ref, conv_ref,
                    *, ts, width, rows_per_step):
    c = lin_ref.shape[-1]
    off = CONV_HIST - (width - 1)
    n_shift = sh_ref.shape[1] - SUBLANES

    @pl.when(pl.program_id(1) == 0)
    def _():
        sh_ref[0, 0:CONV_HIST, :] = jnp.zeros((CONV_HIST, c), F32)

    sh_ref[0, CONV_HIST:CONV_HIST + ts, :] = lin_ref[...].astype(F32) * _sigmoid(gate_ref[...].astype(F32))
    for sft in range(1, SUBLANES):
        sh_ref[sft, 0:n_shift, :] = sh_ref[0, sft:sft + n_shift, :]

    for r0 in range(0, ts, rows_per_step):
        acc = jnp.broadcast_to(cb_ref[...], (rows_per_step, c))
        for k in range(width):
            sft, base = (off + k) % SUBLANES, r0 + (off + k) // SUBLANES * SUBLANES
            w_k = jnp.tile(cw_ref[k], (rows_per_step // SUBLANES, 1))
            acc = acc + w_k * sh_ref[sft, base:base + rows_per_step, :]
        conv_ref[r0:r0 + rows_per_step, :] = acc

    sh_ref[0, 0:CONV_HIST, :] = sh_ref[0, ts:ts + CONV_HIST, :]

    y = _rms(conv_ref[...], cn_ref[...])
    y = y * _sigmoid(y)
    out = jnp.dot(y.astype(BF16), wout_ref[...], preferred_element_type=F32)
    o_ref[...] = (_sigmoid(g_ref[...].astype(F32)) * out).astype(o_ref.dtype)


def mixer_a(z_main, gates, conv_w, conv_b, conv_norm, w_a_out, batch, seq, lin_blk, gate_blk, g_blk, ts=256):
    width, c = conv_w.shape
    d = w_a_out.shape[1]
    ts = _tiles(seq, ts)
    nt = seq // ts
    rows_per_step = min(16, ts)
    assert rows_per_step % SUBLANES == 0
    kern = functools.partial(_mixer_a_kernel, ts=ts, width=width, rows_per_step=rows_per_step)
    row = lambda b, t: b * nt + t
    return pl.pallas_call(
        kern,
        grid=(batch, nt),
        in_specs=[
            pl.BlockSpec((ts, c), lambda b, t: (row(b, t), lin_blk)),
            pl.BlockSpec((ts, c), lambda b, t: (row(b, t), gate_blk)),
            _resident((width, SUBLANES, c)), _resident((1, c)), _resident((1, c)),
            _resident((c, d)),
            pl.BlockSpec((ts, d), lambda b, t: (row(b, t), g_blk)),
        ],
        out_specs=pl.BlockSpec((ts, d), lambda b, t: (row(b, t), 0)),
        out_shape=jax.ShapeDtypeStruct((batch * seq, d), BF16),
        scratch_shapes=[pltpu.VMEM((SUBLANES, ts + CONV_HIST, c), F32), pltpu.VMEM((ts, c), F32)],
        compiler_params=_params("arbitrary", "arbitrary"),
        name="mixer_a",
    )(z_main, z_main, jnp.broadcast_to(conv_w[:, None, :], (width, SUBLANES, c)), conv_b.reshape(1, c),
      conv_norm.reshape(1, c), w_a_out, gates)


def _gelu_tanh(x):
    return 0.5 * x * (1.0 + jnp.tanh(0.7978845608028654 * (x + 0.044715 * (x * x * x))))


def _mixer_b_kernel(u_ref, v_ref, sn_ref, sw_ref, sb_ref, wout_ref, g_ref, yin_ref, o_ref, uv_ref,
                    *, n_blocks, groups, block):
    gd = u_ref.shape[-1] // groups
    v = _rms(_gelu_tanh(v_ref[...].astype(F32)), sn_ref[...]).astype(BF16)
    rc = lax.broadcasted_iota(jnp.int32, (block, block), 0) // CHUNK
    cc = lax.broadcasted_iota(jnp.int32, (block, block), 1) // CHUNK
    causal = rc >= cc
    for g in range(groups):
        w = jnp.where(causal, sw_ref[g], 0.0).astype(BF16)
        bias = sb_ref[:, g:g + 1]
        cols = slice(g * gd, (g + 1) * gd)
        for r in range(n_blocks):
            rows = slice(r * block, (r + 1) * block)
            sv = jnp.dot(w, v[rows, cols], preferred_element_type=F32) + bias
            u = _gelu_tanh(u_ref[rows, cols].astype(F32))
            uv_ref[rows, cols] = (u * sv).astype(BF16)
    out = jnp.dot(uv_ref[...], wout_ref[...], preferred_element_type=F32)
    o_ref[...] = (yin_ref[...].astype(F32) + _sigmoid(g_ref[...].astype(F32)) * out).astype(o_ref.dtype)


def mixer_b(z_main, gates, y_in, sg_norm, sg_w, sg_b, w_b_out, u_blk, v_blk, g_blk, ts=512):
    groups, block, _ = sg_w.shape
    c, d = w_b_out.shape
    t = z_main.shape[0]
    ts = max(_tiles(t, ts), block)
    kern = functools.partial(_mixer_b_kernel, n_blocks=ts // block, groups=groups, block=block)
    return pl.pallas_call(
        kern,
        grid=(t // ts,),
        in_specs=[
            pl.BlockSpec((ts, c), lambda i: (i, u_blk)),
            pl.BlockSpec((ts, c), lambda i: (i, v_blk)),
            _resident((1, c)), _resident((groups, block, block)), _resident((block, groups)),
            _resident((c, d)),
            pl.BlockSpec((ts, d), lambda i: (i, g_blk)),
            pl.BlockSpec((ts, d), lambda i: (i, 0)),
        ],
        out_specs=pl.BlockSpec((ts, d), lambda i: (i, 0)),
        out_shape=jax.ShapeDtypeStruct((t, d), BF16),
        scratch_shapes=[pltpu.VMEM((ts, c), BF16)],
        compiler_params=_params("parallel"),
        name="mixer_b",
    )(z_main, z_main, sg_norm.reshape(1, c), sg_w, jnp.transpose(sg_b), w_b_out, gates, y_in)


def _rope_group(x, cos_ref, sa_ref, sb_ref):
    half = QK_ROPE // 2
    return (x * cos_ref[...] + pltpu.roll(x, LANES - half, 1) * sa_ref[...]
            + pltpu.roll(x, half, 1) * sb_ref[...])


def _norm_group(x, gain, n):
    ms = jnp.sum(x * x, axis=-1, keepdims=True) * (1.0 / n)
    return x * lax.rsqrt(ms + EPS) * gain


def _q_proj_kernel(cq_ref, qn_ref, wq_ref, gn_ref, gr_ref, cos_ref, sa_ref, sb_ref, o_ref, *, heads):
    h = _rms(cq_ref[...].astype(F32), qn_ref[...]).astype(BF16)
    q = jnp.dot(h, wq_ref[...], preferred_element_type=F32)
    scale = QK_DIM ** -0.5 * LOG2_E
    for hd in range(heads):
        c0 = hd * HEAD_PAD
        nope = _norm_group(q[:, c0:c0 + QK_NOPE], gn_ref[...], QK_NOPE)
        rope = _norm_group(q[:, c0 + QK_NOPE:c0 + HEAD_PAD], gr_ref[...], QK_ROPE)
        rope = _rope_group(rope, cos_ref, sa_ref, sb_ref)
        o_ref[:, c0:c0 + QK_NOPE] = (nope * scale).astype(o_ref.dtype)
        o_ref[:, c0 + QK_NOPE:c0 + HEAD_PAD] = (rope * scale).astype(o_ref.dtype)


def _kv_proj_kernel(ckv_ref, kr_ref, kvn_ref, wkv_ref, gn_ref, gr_ref, cos_ref, sa_ref, sb_ref,
                    k_ref, v_ref, *, heads):
    h = _rms(ckv_ref[...].astype(F32), kvn_ref[...]).astype(BF16)
    kv = jnp.dot(h, wkv_ref[...], preferred_element_type=F32)
    kr = kr_ref[...].astype(F32)
    kr = jnp.where(lax.broadcasted_iota(jnp.int32, kr.shape, 1) < QK_ROPE, kr, 0.0)
    rope = _norm_group(kr, gr_ref[...], QK_ROPE)
    rope = _rope_group(rope, cos_ref, sa_ref, sb_ref).astype(k_ref.dtype)
    for hd in range(heads):
        c0 = hd * (QK_NOPE + V_DIM)
        nope = _norm_group(kv[:, c0:c0 + QK_NOPE], gn_ref[...], QK_NOPE)
        k_ref[:, hd * HEAD_PAD:hd * HEAD_PAD + QK_NOPE] = nope.astype(k_ref.dtype)
        k_ref[:, hd * HEAD_PAD + QK_NOPE:(hd + 1) * HEAD_PAD] = rope
        v_ref[:, 2 * hd * V_DIM:(2 * hd + 1) * V_DIM] = kv[:, c0 + QK_NOPE:c0 + QK_NOPE + V_DIM].astype(v_ref.dtype)
        v_ref[:, (2 * hd + 1) * V_DIM:(2 * hd + 2) * V_DIM] = jnp.ones((kv.shape[0], V_DIM), v_ref.dtype)


def _pad_gain(gain):
    gn = gain[:QK_NOPE].reshape(1, QK_NOPE)
    gr = jnp.concatenate([gain[QK_NOPE:], jnp.zeros((LANES - QK_ROPE,), gain.dtype)]).reshape(1, LANES)
    return gn, gr


def q_proj(z_main, cq_blk, q_norm, w_uq_pad, q_gain, tabs, heads, tm=256):
    t = z_main.shape[0]
    ql = q_norm.shape[0]
    tm = _tiles(t, tm)
    gn, gr = _pad_gain(q_gain)
    tab_spec = pl.BlockSpec((tm, LANES), lambda i: (i, 0))
    return pl.pallas_call(
        functools.partial(_q_proj_kernel, heads=heads),
        grid=(t // tm,),
        in_specs=[pl.BlockSpec((tm, ql), lambda i: (i, cq_blk)), _resident((1, ql)),
                  _resident((ql, heads * HEAD_PAD)), _resident((1, QK_NOPE)), _resident((1, LANES)),
                  tab_spec, tab_spec, tab_spec],
        out_specs=pl.BlockSpec((tm, heads * HEAD_PAD), lambda i: (i, 0)),
        out_shape=jax.ShapeDtypeStruct((t, heads * HEAD_PAD), BF16),
        compiler_params=_params("parallel"),
        name="q_proj",
    )(z_main, q_norm.reshape(1, ql), w_uq_pad, gn, gr, *tabs)


def kv_proj(z_main, ckv_blk, k_rope, kv_norm, w_ukv, k_gain, tabs, heads, tm=256):
    t = z_main.shape[0]
    kl = kv_norm.shape[0]
    tm = _tiles(t, tm)
    gn, gr = _pad_gain(k_gain)
    tab_spec = pl.BlockSpec((tm, LANES), lambda i: (i, 0))
    return pl.pallas_call(
        functools.partial(_kv_proj_kernel, heads=heads),
        grid=(t // tm,),
        in_specs=[pl.BlockSpec((tm, kl), lambda i: (i, ckv_blk)), tab_spec, _resident((1, kl)),
                  _resident((kl, heads * (QK_NOPE + V_DIM))), _resident((1, QK_NOPE)), _resident((1, LANES)),
                  tab_spec, tab_spec, tab_spec],
        out_specs=[pl.BlockSpec((tm, heads * HEAD_PAD), lambda i: (i, 0)),
                   pl.BlockSpec((tm, heads * 2 * V_DIM), lambda i: (i, 0))],
        out_shape=[jax.ShapeDtypeStruct((t, heads * HEAD_PAD), BF16),
                   jax.ShapeDtypeStruct((t, heads * 2 * V_DIM), BF16)],
        compiler_params=_params("parallel"),
        name="kv_proj",
    )(z_main, k_rope, kv_norm.reshape(1, kl), w_ukv, gn, gr, *tabs)


def _attn_kernel(q_ref, k_ref, v_ref, o_ref, m_ref, acc_ref, p_ref, a_ref, *, tq, tk, heads_per_step):
    i = pl.program_id(2)
    wv = 2 * V_DIM
    per_q = tq // tk
    assert per_q == 2, "the slot ping-pong below is written for two key tiles per query tile"

    m_ref[...] = jnp.full(m_ref.shape, NEG_BIG, F32)
    acc_ref[...] = jnp.zeros(acc_ref.shape, F32)
    p_ref[1] = jnp.zeros(p_ref.shape[1:], BF16)
    a_ref[1] = jnp.ones(a_ref.shape[1:], F32)

    def scores(j, masked):
        k0 = pl.multiple_of(j * tk, tk)
        out = []
        for hd in range(heads_per_step):
            q = q_ref[:, hd * HEAD_PAD:(hd + 1) * HEAD_PAD]
            k = k_ref[pl.ds(k0, tk), hd * HEAD_PAD:(hd + 1) * HEAD_PAD]
            s = lax.dot_general(q, k, (((1,), (1,)), ((), ())), preferred_element_type=F32)
            if masked:
                q_chunk = (i * tq + lax.broadcasted_iota(jnp.int32, (tq, tk), 0)) // CHUNK
                k_chunk = (k0 + lax.broadcasted_iota(jnp.int32, (tq, tk), 1)) // CHUNK
                s = jnp.where(q_chunk >= k_chunk, s, NEG_BIG)
            out.append(s)
        return out

    def probs(s_heads, slot):
        for hd, s in enumerate(s_heads):
            m_prev = m_ref[hd]
            m_new = jnp.maximum(m_prev, jnp.max(s, axis=-1, keepdims=True))
            p_ref[slot, hd] = jnp.exp2(s - jnp.tile(m_new, (1, tk // LANES))).astype(BF16)
            a_ref[slot, hd] = jnp.exp2(m_prev - m_new)
            m_ref[hd] = m_new

    def accumulate(j, slot):
        k0 = pl.multiple_of(jnp.maximum(j, 0) * tk, tk)
        for hd in range(heads_per_step):
            pv = jnp.dot(p_ref[slot, hd], v_ref[pl.ds(k0, tk), hd * wv:(hd + 1) * wv], preferred_element_type=F32)
            acc_ref[hd] = jnp.tile(a_ref[slot, hd], (1, wv // LANES)) * acc_ref[hd] + pv

    def tile_pair(j, masked):
        accumulate(j - 1, 1)
        s_even = scores(j, masked)
        s_odd = scores(j + 1, masked)
        probs(s_even, 0)
        accumulate(j, 0)
        probs(s_odd, 1)

    def body(t, carry):
        tile_pair(2 * t, False)
        return carry

    lax.fori_loop(0, i, body, 0)
    tile_pair(2 * i, True)
    accumulate(2 * i + 1, 1)
    for hd in range(heads_per_step):
        acc = acc_ref[hd]
        o_ref[:, hd * V_DIM:(hd + 1) * V_DIM] = (acc[:, :V_DIM] / acc[:, V_DIM:V_DIM + 1]).astype(o_ref.dtype)


def attention(q, k, v, batch, seq, heads, tq=1024, heads_per_step=2):
    tq = _tiles(seq, tq)
    tk = tq // 2
    hps = heads_per_step if heads % heads_per_step == 0 else 1
    q3 = q.reshape(batch, seq, heads * HEAD_PAD)
    k3 = k.reshape(batch, seq, heads * HEAD_PAD)
    v3 = v.reshape(batch, seq, heads * 2 * V_DIM)
    out = pl.pallas_call(
        functools.partial(_attn_kernel, tq=tq, tk=tk, heads_per_step=hps),
        grid=(batch, heads // hps, seq // tq),
        in_specs=[pl.BlockSpec((None, tq, hps * HEAD_PAD), lambda b, h, i: (b, i, h)),
                  pl.BlockSpec((None, seq, hps * HEAD_PAD), lambda b, h, i: (b, 0, h)),
                  pl.BlockSpec((None, seq, hps * 2 * V_DIM), lambda b, h, i: (b, 0, h))],
        out_specs=pl.BlockSpec((None, tq, hps * V_DIM), lambda b, h, i: (b, i, h)),
        out_shape=jax.ShapeDtypeStruct((batch, seq, heads * V_DIM), BF16),
        scratch_shapes=[pltpu.VMEM((hps, tq, LANES), F32), pltpu.VMEM((hps, tq, 2 * V_DIM), F32),
                        pltpu.VMEM((2, hps, tq, tk), BF16), pltpu.VMEM((2, hps, tq, LANES), F32)],
        compiler_params=_params("parallel", "parallel", "arbitrary"),
        name="attention",
    )(q3, k3, v3)
    return out.reshape(batch * seq, heads * V_DIM)


def _gated_out_kernel(a_ref, w_ref, g_ref, yin_ref, o_ref):
    out = jnp.dot(a_ref[...], w_ref[...], preferred_element_type=F32)
    o_ref[...] = (yin_ref[...].astype(F32) + _sigmoid(g_ref[...].astype(F32)) * out).astype(o_ref.dtype)


def gated_out(a, w, gates, g_blk, y_in, tm=512):
    t, k = a.shape
    d = w.shape[1]
    tm = _tiles(t, tm)
    return pl.pallas_call(
        _gated_out_kernel,
        grid=(t // tm,),
        in_specs=[pl.BlockSpec((tm, k), lambda i: (i, 0)), _resident((k, d)),
                  pl.BlockSpec((tm, d), lambda i: (i, g_blk)), pl.BlockSpec((tm, d), lambda i: (i, 0))],
        out_specs=pl.BlockSpec((tm, d), lambda i: (i, 0)),
        out_shape=jax.ShapeDtypeStruct((t, d), BF16),
        compiler_params=_params("parallel"),
        name="gated_out",
    )(a, w, gates, y_in)


def _router_kernel(x_ref, g_ref, wr_ref, br_ref, h_ref, w_ref, id_ref, *, n_groups, per_group):
    h = _rms(x_ref[...].astype(F32), g_ref[...])
    h_ref[...] = h.astype(h_ref.dtype)
    logits = jnp.dot(h, wr_ref[...], precision=lax.Precision.HIGHEST, preferred_element_type=F32) + br_ref[...]
    n_exp = n_groups * per_group
    lane = lax.broadcasted_iota(jnp.int32, logits.shape, 1)
    lane_f = lane.astype(F32)
    far = float(LANES)

    def softmax_where(mask):
        z = jnp.where(mask, logits, -jnp.inf)
        e = jnp.exp(z - jnp.max(z, axis=-1, keepdims=True))
        return jnp.where(mask, e / jnp.sum(e, axis=-1, keepdims=True), -1.0)

    def top1(p):
        best = jnp.max(p, axis=-1, keepdims=True)
        idx = jnp.min(jnp.where(p == best, lane_f, far), axis=-1, keepdims=True)
        return best, idx

    g_prob = softmax_where((lane >= n_exp) & (lane < n_exp + n_groups))
    g_w, g_lane = top1(g_prob)
    lo = (g_lane - float(n_exp)) * float(per_group)
    e_prob = softmax_where((lane_f >= lo) & (lane_f < lo + float(per_group)))
    p1, i1 = top1(e_prob)
    p2, i2 = top1(jnp.where(lane_f == i1, -1.0, e_prob))
    den = p1 + p2
    w_ref[...] = jnp.where(lane == 0, g_w * (p1 / den), jnp.where(lane == 1, g_w * (p2 / den), 0.0))
    id_ref[...] = jnp.where(lane == 0, i1, jnp.where(lane == 1, i2, 0.0)).astype(jnp.int32)


def router(x, ffn_norm, w_rg, b_rg, w_re, b_re, tm=256):
    t, d = x.shape
    n_groups, n_exp = w_rg.shape[1], w_re.shape[1]
    tm = _tiles(t, tm)
    pad = jnp.zeros((d, LANES - n_exp - n_groups), F32)
    wr = jnp.concatenate([w_re.astype(F32), w_rg.astype(F32), pad], axis=1)
    br = jnp.concatenate([b_re.astype(F32), b_rg.astype(F32), pad[0]]).reshape(1, LANES)
    h, w, ids = pl.pallas_call(
        functools.partial(_router_kernel, n_groups=n_groups, per_group=n_exp // n_groups),
        grid=(t // tm,),
        in_specs=[pl.BlockSpec((tm, d), lambda i: (i, 0)), _resident((1, d)), _resident((d, LANES)),
                  _resident((1, LANES))],
        out_specs=[pl.BlockSpec((tm, d), lambda i: (i, 0)), pl.BlockSpec((tm, LANES), lambda i: (i, 0)),
                   pl.BlockSpec((tm, LANES), lambda i: (i, 0))],
        out_shape=[jax.ShapeDtypeStruct((t, d), BF16), jax.ShapeDtypeStruct((t, LANES), F32),
                   jax.ShapeDtypeStruct((t, LANES), jnp.int32)],
        compiler_params=_params("parallel"),
        name="router",
    )(x, ffn_norm.reshape(1, d), wr, br)
    return h, w, ids


def _expert_kernel(te_ref, na_ref, x_ref, wg_ref, wu_ref, wd_ref, o_ref):
    del te_ref
    active = pl.program_id(0) < na_ref[0]

    @pl.when(active)
    def _():
        x = x_ref[...]
        g = jnp.dot(x, wg_ref[...], preferred_element_type=F32)
        u = jnp.dot(x, wu_ref[...], preferred_element_type=F32)
        hid = (g * _sigmoid(g) * u).astype(BF16)
        o_ref[...] = jnp.dot(hid, wd_ref[...], preferred_element_type=F32).astype(o_ref.dtype)

    @pl.when(jnp.logical_not(active))
    def _():
        o_ref[...] = jnp.zeros(o_ref.shape, o_ref.dtype)


def expert_ffn(xs, tile_expert, n_active, w_gate, w_up, w_down, tile):
    p, d = xs.shape
    f = w_gate.shape[-1]
    grid_spec = pltpu.PrefetchScalarGridSpec(
        num_scalar_prefetch=2,
        grid=(p // tile,),
        in_specs=[pl.BlockSpec((tile, d), lambda i, te, na: (i, 0)),
                  pl.BlockSpec((None, d, f), lambda i, te, na: (te[i], 0, 0)),
                  pl.BlockSpec((None, d, f), lambda i, te, na: (te[i], 0, 0)),
                  pl.BlockSpec((None, f, d), lambda i, te, na: (te[i], 0, 0))],
        out_specs=pl.BlockSpec((tile, d), lambda i, te, na: (i, 0)),
    )
    return pl.pallas_call(
        _expert_kernel,
        grid_spec=grid_spec,
        out_shape=jax.ShapeDtypeStruct((p, d), BF16),
        compiler_params=_params("arbitrary"),
        name="expert_ffn",
    )(tile_expert, n_active, xs, w_gate, w_up, w_down)


def _combine_kernel(x_ref, w_ref, a_ref, b_ref, o_ref):
    w = w_ref[...]
    o_ref[...] = x_ref[...] + w[:, 0:1] * a_ref[...].astype(F32) + w[:, 1:2] * b_ref[...].astype(F32)


def combine(x, w, r0, r1, tm=256):
    t, d = x.shape
    tm = _tiles(t, tm)
    spec = pl.BlockSpec((tm, d), lambda i: (i, 0))
    return pl.pallas_call(
        _combine_kernel, grid=(t // tm,),
        in_specs=[spec, pl.BlockSpec((tm, LANES), lambda i: (i, 0)), spec, spec], out_specs=spec,
        out_shape=jax.ShapeDtypeStruct((t, d), x.dtype), compiler_params=_params("parallel"), name="combine",
    )(x, w, r0, r1)


def _cast_stream_kernel(x_hbm, o_hbm, ibuf, obuf, isem, osem, *, row0, rows, n_chunks, depth):
    def in_copy(c, slot):
        return pltpu.make_async_copy(x_hbm.at[pl.ds(row0 + c * rows, rows)], ibuf.at[slot], isem.at[slot])

    def out_copy(c, slot):
        return pltpu.make_async_copy(obuf.at[slot], o_hbm.at[pl.ds(c * rows, rows)], osem.at[slot])

    for c in range(min(depth, n_chunks)):
        in_copy(c, c).start()

    def body(c, carry):
        slot = lax.rem(c, depth)
        in_copy(c, slot).wait()

        @pl.when(c >= depth)
        def _():
            out_copy(c - depth, slot).wait()

        obuf[slot] = ibuf[slot].astype(obuf.dtype)
        out_copy(c, slot).start()

        @pl.when(c + depth < n_chunks)
        def _():
            in_copy(c + depth, slot).start()

        return carry

    lax.fori_loop(0, n_chunks, body, 0)
    for c in range(max(n_chunks - depth, 0), n_chunks):
        out_copy(c, c % depth).wait()


def cast_layer(w, layer, out_dtype=BF16):
    per_layer = w.shape[1:]
    b = per_layer[-1]
    n_rows = 1
    for dim in per_layer[:-1]:
        n_rows *= dim
    rows = _tiles(n_rows, max(BF16_SUBLANES, CAST_CHUNK_BYTES // (b * w.dtype.itemsize)))
    kern = functools.partial(_cast_stream_kernel, row0=layer * n_rows, rows=rows, n_chunks=n_rows // rows,
                             depth=CAST_DEPTH)
    out = pl.pallas_call(
        kern,
        in_specs=[pl.BlockSpec(memory_space=pl.ANY)],
        out_specs=pl.BlockSpec(memory_space=pl.ANY),
        out_shape=jax.ShapeDtypeStruct((n_rows, b), out_dtype),
        scratch_shapes=[pltpu.VMEM((CAST_DEPTH, rows, b), w.dtype), pltpu.VMEM((CAST_DEPTH, rows, b), out_dtype),
                        pltpu.SemaphoreType.DMA((CAST_DEPTH,)), pltpu.SemaphoreType.DMA((CAST_DEPTH,))],
        compiler_params=pltpu.CompilerParams(vmem_limit_bytes=VMEM_LIMIT_BYTES),
        name="cast_layer",
    )(w.reshape(w.shape[0] * n_rows, b))
    return out.reshape(per_layer)


def _dispatch_plan(ids, n_exp, tile, n_tiles):
    n_assign = ids.size
    e_flat = ids.reshape(-1)
    iota = jnp.arange(n_assign, dtype=jnp.int32)
    e_sorted, order = lax.sort_key_val(e_flat, iota)
    experts = jnp.arange(n_exp, dtype=jnp.int32)
    counts = jnp.sum((e_flat[:, None] == experts[None, :]).astype(jnp.int32), axis=0)
    starts = jnp.cumsum(counts) - counts
    padded = (counts + tile - 1) // tile * tile
    pad_ends = jnp.cumsum(padded)
    pad_starts = pad_ends - padded
    shift = pad_starts - starts
    dest_sorted = iota + jnp.sum(jnp.where(e_sorted[:, None] == experts[None, :], shift[None, :], 0), axis=1)
    _, pos = lax.sort_key_val(order, dest_sorted)
    n_active = pad_ends[-1] // tile
    tile_idx = jnp.minimum(jnp.arange(n_tiles, dtype=jnp.int32), n_active - 1)
    tile_expert = jnp.sum((pad_ends[None, :] <= (tile_idx * tile)[:, None]).astype(jnp.int32), axis=1)
    tile_expert = jnp.minimum(tile_expert, n_exp - 1)
    slot = jnp.arange(n_tiles * tile, dtype=jnp.int32)
    slot_expert = jnp.repeat(tile_expert, tile)
    rank = slot - pad_starts[slot_expert]
    valid = rank < counts[slot_expert]
    src_assign = order.at[jnp.clip(starts[slot_expert] + rank, 0, n_assign - 1)].get(mode="promise_in_bounds")
    src_tok = jnp.where(valid, src_assign // TOP_K, 0)
    return src_tok, pos.reshape(-1, TOP_K), tile_expert, n_active.reshape(1).astype(jnp.int32)


def _take_rows(x, idx):
    return x.at[idx].get(mode="promise_in_bounds")


def hier_moe(x, ffn_norm, w_rg, b_rg, w_re, b_re, w_gate, w_up, w_down, tile=256):
    t = x.shape[0]
    n_exp = w_gate.shape[0]
    tile = _tiles(t, tile)
    h, wts, ids = router(x, ffn_norm, w_rg, b_rg, w_re, b_re)
    n_tiles = -(-(t * TOP_K + n_exp * (tile - 1)) // tile)
    src_tok, pos, tile_expert, n_active = _dispatch_plan(ids[:, :TOP_K], n_exp, tile, n_tiles)
    rows = expert_ffn(_take_rows(h, src_tok), tile_expert, n_active, w_gate, w_up, w_down, tile)
    return combine(x, wts, _take_rows(rows, pos[:, 0]), _take_rows(rows, pos[:, 1]))


def _rope_tables(positions):
    half = QK_ROPE // 2
    inv = jnp.power(ROPE_BASE, -jnp.arange(0, QK_ROPE, 2, dtype=F32) / QK_ROPE)
    ang = positions.astype(F32)[..., None] * inv
    cos = jnp.cos(ang).reshape(-1, half)
    sin = jnp.sin(ang).reshape(-1, half)
    z = jnp.zeros_like(cos)
    zz = jnp.zeros((cos.shape[0], LANES - QK_ROPE), F32)
    return (jnp.concatenate([cos, cos, zz], axis=1), jnp.concatenate([-sin, z, zz], axis=1),
            jnp.concatenate([z, sin, zz], axis=1))


def mixer_block(x, tabs, batch, seq, layer, mix_norm, w_in_all, conv_w, conv_b, conv_norm, w_a_out_all, sg_norm,
                sg_w, sg_b, w_b_out_all, q_norm, w_uq, kv_norm, w_ukv_all, q_gain, k_gain, w_c_out_all, w_out_all):
    d = x.shape[1]
    c_conv = conv_w.shape[1]
    c_sg = sg_norm.shape[0]
    q_lora, kv_lora = q_norm.shape[0], kv_norm.shape[0]
    heads = w_uq.shape[1] // QK_DIM
    i0 = 2 * c_conv
    i1 = i0 + 2 * c_sg
    i2 = i1 + q_lora
    i3 = i2 + kv_lora
    i4 = i3 + QK_ROPE
    assert c_conv == c_sg and i1 % q_lora == 0 and i2 % kv_lora == 0, "column-block layout of the input projection"

    h = rmsnorm_rows(x, mix_norm)
    z_main = matmul_wf32(h, w_in_all, layer, 0, i3, BF16, name="in_proj_main")
    k_rope = matmul_wf32(h, w_in_all, layer, i3, LANES, BF16, name="in_proj_rope")
    gates = matmul_wf32(h, w_in_all, layer, i3, w_in_all.shape[-1] - i4, BF16, shift=i4 - i3, name="in_proj_gates")

    y = mixer_a(z_main, gates, conv_w, conv_b, conv_norm, cast_layer(w_a_out_all, layer), batch, seq, 0, 1, 0)
    y = mixer_b(z_main, gates, y, sg_norm, sg_w, sg_b, cast_layer(w_b_out_all, layer), 2, 3, 1)

    w_uq_pad = jnp.pad(w_uq.reshape(q_lora, heads, QK_DIM), ((0, 0), (0, 0), (0, HEAD_PAD - QK_DIM)))
    w_uq_pad = w_uq_pad.reshape(q_lora, heads * HEAD_PAD).astype(BF16)
    q = q_proj(z_main, i1 // q_lora, q_norm, w_uq_pad, q_gain, tabs, heads)
    k, v = kv_proj(z_main, i2 // kv_lora, k_rope, kv_norm, cast_layer(w_ukv_all, layer), k_gain, tabs, heads)
    o = attention(q, k, v, batch, seq, heads)
    y = gated_out(o, cast_layer(w_c_out_all, layer), gates, 2, y)

    return matmul_wf32(y, w_out_all, layer, 0, d, F32, residual=x, name="out_proj")


def kernel(x, positions, mix_norm, w_in, conv_w, conv_b, conv_norm, w_a_out, sg_norm, sg_w, sg_b, w_b_out, q_norm, w_uq, kv_norm, w_ukv, q_gain, k_gain, w_c_out, w_out, ffn_norm, w_router_group, b_router_group, w_router_expert, b_router_expert, w_gate, w_up, w_down):
    batch, seq, d = x.shape
    tabs = _rope_tables(positions)
    xt = x.reshape(batch * seq, d)
    for l in range(mix_norm.shape[0]):
        xt = mixer_block(xt, tabs, batch, seq, l, mix_norm[l], w_in, conv_w[l], conv_b[l], conv_norm[l], w_a_out,
                         sg_norm[l], sg_w[l], sg_b[l], w_b_out, q_norm[l], w_uq[l], kv_norm[l], w_ukv,
                         q_gain[l], k_gain[l], w_c_out, w_out)
        xt = hier_moe(xt, ffn_norm[l], w_router_group[l], b_router_group[l], w_router_expert[l],
                      b_router_expert[l], cast_layer(w_gate, l), cast_layer(w_up, l), cast_layer(w_down, l))
    return xt.reshape(batch, seq, d)
```

```python
import functools

import jax
import jax.numpy as jnp
from jax import lax
from jax.experimental import pallas as pl
from jax.experimental.pallas import tpu as pltpu

CHUNK = 64
EPS = 1e-6
QK_NOPE = 128
QK_ROPE = 64
V_DIM = 128
QK_DIM = QK_NOPE + QK_ROPE
ROPE_BASE = 10000.0
TOP_K = 2

LANES = 128
SUBLANES = 8
BF16_SUBLANES = 16
HEAD_PAD = 2 * LANES
VMEM_LIMIT_BYTES = 56 * 1024 * 1024
CONV_HIST = 32
CAST_CHUNK_BYTES = 1024 * 1024
CAST_DEPTH = 8
NEG_BIG = -1e30
LOG2_E = 1.4426950408889634

F32 = jnp.float32
BF16 = jnp.bfloat16


def _tiles(n, pref):
    if n <= pref:
        return n
    for t in range(pref - pref % BF16_SUBLANES, 0, -BF16_SUBLANES):
        if n % t == 0:
            return t
    return n


def _params(*sem):
    return pltpu.CompilerParams(dimension_semantics=sem, vmem_limit_bytes=VMEM_LIMIT_BYTES)


def _resident(shape):
    nd = len(shape)
    return pl.BlockSpec(shape, lambda *_: (0,) * nd, pipeline_mode=pl.Buffered(1))


def _rms(xf, g):
    ms = jnp.mean(xf * xf, axis=-1, keepdims=True)
    return xf * lax.rsqrt(ms + EPS) * g


def _sigmoid(x):
    return 1.0 / (1.0 + jnp.exp(-x))


def _rmsnorm_kernel(x_ref, g_ref, o_ref):
    o_ref[...] = _rms(x_ref[...].astype(F32), g_ref[...]).astype(o_ref.dtype)


def rmsnorm_rows(x, g, out_dtype=BF16, tm=512):
    t, d = x.shape
    tm = _tiles(t, tm)
    return pl.pallas_call(
        _rmsnorm_kernel,
        grid=(t // tm,),
        in_specs=[pl.BlockSpec((tm, d), lambda i: (i, 0)), _resident((1, d))],
        out_specs=pl.BlockSpec((tm, d), lambda i: (i, 0)),
        out_shape=jax.ShapeDtypeStruct((t, d), out_dtype),
        compiler_params=_params("parallel"),
        name="rmsnorm_rows",
    )(x, g.reshape(1, d))


def _matmul_kernel(a_ref, b_ref, o_ref):
    o_ref[...] = jnp.dot(a_ref[...], b_ref[...], preferred_element_type=F32).astype(o_ref.dtype)


def _matmul_res_kernel(a_ref, b_ref, r_ref, o_ref):
    acc = jnp.dot(a_ref[...], b_ref[...], preferred_element_type=F32)
    o_ref[...] = (r_ref[...].astype(F32) + acc).astype(o_ref.dtype)


def matmul(a, b, out_dtype, residual=None, tm=1024, tn=1024, name="matmul"):
    m, k = a.shape
    n = b.shape[1]
    tm, tn = _tiles(m, tm), _tiles(n, tn)
    in_specs = [pl.BlockSpec((tm, k), lambda i, j: (i, 0)), pl.BlockSpec((k, tn), lambda i, j: (0, j))]
    args = [a, b]
    kern = _matmul_kernel
    if residual is not None:
        in_specs.append(pl.BlockSpec((tm, tn), lambda i, j: (i, j)))
        args.append(residual)
        kern = _matmul_res_kernel
    return pl.pallas_call(
        kern,
        grid=(m // tm, n // tn),
        in_specs=in_specs,
        out_specs=pl.BlockSpec((tm, tn), lambda i, j: (i, j)),
        out_shape=jax.ShapeDtypeStruct((m, n), out_dtype),
        compiler_params=_params("parallel", "arbitrary"),
        name=name,
    )(*args)


def _matmul_wf32_kernel(*refs, shift, k_chunk, has_res):
    a_ref, w_ref = refs[0], refs[1]
    rest = list(refs[2:])
    wn_ref = rest.pop(0) if shift else None
    r_ref = rest.pop(0) if has_res else None
    o_ref, wb_ref = rest
    k = w_ref.shape[0]

    @pl.when(pl.program_id(1) == 0)
    def _():
        for k0 in range(0, k, k_chunk):
            w = w_ref[k0:k0 + k_chunk, :]
            if shift:
                w = jnp.concatenate([w[:, shift:], wn_ref[k0:k0 + k_chunk, :shift]], axis=1)
            wb_ref[k0:k0 + k_chunk, :] = w.astype(BF16)

    acc = jnp.dot(a_ref[...], wb_ref[...], preferred_element_type=F32)
    if has_res:
        acc = acc + r_ref[...].astype(F32)
    o_ref[...] = acc.astype(o_ref.dtype)


def matmul_wf32(a, w_all, layer, col0, n, out_dtype, shift=0, residual=None, tm=1024, tn=512, name="matmul_wf32"):
    m, k = a.shape
    tm = _tiles(m, tm)
    tn = max(t for t in range(LANES, max(tn, LANES) + 1, LANES) if n % t == 0 and col0 % t == 0)
    assert 0 <= shift < LANES
    c0 = col0 // tn
    in_specs = [pl.BlockSpec((tm, k), lambda j, i: (i, 0)),
                pl.BlockSpec((None, k, tn), lambda j, i: (layer, 0, c0 + j), pipeline_mode=pl.Buffered(1))]
    args = [a, w_all]
    if shift:
        in_specs.append(pl.BlockSpec((None, k, LANES), lambda j, i: (layer, 0, (c0 + j + 1) * (tn // LANES)),
                                     pipeline_mode=pl.Buffered(1)))
        args.append(w_all)
    if residual is not None:
        in_specs.append(pl.BlockSpec((tm, tn), lambda j, i: (i, j)))
        args.append(residual)
    kern = functools.partial(_matmul_wf32_kernel, shift=shift, k_chunk=_tiles(k, 512), has_res=residual is not None)
    return pl.pallas_call(
        kern,
        grid=(n // tn, m // tm),
        in_specs=in_specs,
        out_specs=pl.BlockSpec((tm, tn), lambda j, i: (i, j)),
        out_shape=jax.ShapeDtypeStruct((m, n), out_dtype),
        scratch_shapes=[pltpu.VMEM((k, tn), BF16)],
        compiler_params=_params("arbitrary", "arbitrary"),
        name=name,
    )(*args)


def _matmul_wt_kernel(a0_ref, a1_ref, w_ref, o_ref):
    kh = a0_ref.shape[1]
    w = w_ref[0].astype(BF16)
    nt = (((1,), (1,)), ((), ()))
    acc = lax.dot_general(a0_ref[...], w[:, :kh], nt, preferred_element_type=F32)
    acc = acc + lax.dot_general(a1_ref[...], w[:, kh:], nt, preferred_element_type=F32)
    o_ref[...] = acc.astype(o_ref.dtype)


def matmul_wt(a, w_t_all, layer, row0, n, out_dtype, tm=2048, tn=512, name="matmul_wt"):
    m, k = a.shape
    tm = _tiles(m, tm)
    tn = max(t for t in range(LANES, max(tn, LANES) + 1, LANES) if n % t == 0)
    assert row0 % SUBLANES == 0 and k % (2 * LANES) == 0
    half = pl.BlockSpec((tm, k // 2), lambda i, j: (i, 0), pipeline_mode=pl.Buffered(1))
    half_hi = pl.BlockSpec((tm, k // 2), lambda i, j: (i, 1), pipeline_mode=pl.Buffered(1))
    return pl.pallas_call(
        _matmul_wt_kernel,
        grid=(m // tm, n // tn),
        in_specs=[half, half_hi,
                  pl.BlockSpec((pl.Element(1), pl.Element(tn), pl.Element(k)),
                               lambda i, j: (layer, pl.multiple_of(row0 + j * tn, SUBLANES), 0))],
        out_specs=pl.BlockSpec((tm, tn), lambda i, j: (i, j)),
        out_shape=jax.ShapeDtypeStruct((m, n), out_dtype),
        compiler_params=_params("parallel", "arbitrary"),
        name=name,
    )(a, a, w_t_all)


def _mixer_a_kernel(lin_ref, gate_ref, cw_ref, cb_ref, cn_ref, wout_ref, g_ref, o_ref, sh_ref, conv_ref,
                    *, ts, width, rows_per_step):
    c = lin_ref.shape[-1]
    off = CONV_HIST - (width - 1)
    n_shift = sh_ref.shape[1] - SUBLANES

    @pl.when(pl.program_id(1) == 0)
    def _():
        sh_ref[0, 0:CONV_HIST, :] = jnp.zeros((CONV_HIST, c), F32)

    sh_ref[0, CONV_HIST:CONV_HIST + ts, :] = lin_ref[...].astype(F32) * _sigmoid(gate_ref[...].astype(F32))
    for sft in range(1, SUBLANES):
        sh_ref[sft, 0:n_shift, :] = sh_ref[0, sft:sft + n_shift, :]

    for r0 in range(0, ts, rows_per_step):
        acc = jnp.broadcast_to(cb_ref[...], (rows_per_step, c))
        for k in range(width):
            sft, base = (off + k) % SUBLANES, r0 + (off + k) // SUBLANES * SUBLANES
            w_k = jnp.tile(cw_ref[k], (rows_per_step // SUBLANES, 1))
            acc = acc + w_k * sh_ref[sft, base:base + rows_per_step, :]
        conv_ref[r0:r0 + rows_per_step, :] = acc

    sh_ref[0, 0:CONV_HIST, :] = sh_ref[0, ts:ts + CONV_HIST, :]

    y = _rms(conv_ref[...], cn_ref[...])
    y = y * _sigmoid(y)
    out = jnp.dot(y.astype(BF16), wout_ref[...], preferred_element_type=F32)
    o_ref[...] = (_sigmoid(g_ref[...].astype(F32)) * out).astype(o_ref.dtype)


def mixer_a(z_main, gates, conv_w, conv_b, conv_norm, w_a_out, batch, seq, lin_blk, gate_blk, g_blk, ts=256):
    width, c = conv_w.shape
    d = w_a_out.shape[1]
    ts = _tiles(seq, ts)
    nt = seq // ts
    rows_per_step = min(16, ts)
    assert rows_per_step % SUBLANES == 0
    kern = functools.partial(_mixer_a_kernel, ts=ts, width=width, rows_per_step=rows_per_step)
    row = lambda b, t: b * nt + t
    return pl.pallas_call(
        kern,
        grid=(batch, nt),
        in_specs=[
            pl.BlockSpec((ts, c), lambda b, t: (row(b, t), lin_blk)),
            pl.BlockSpec((ts, c), lambda b, t: (row(b, t), gate_blk)),
            _resident((width, SUBLANES, c)), _resident((1, c)), _resident((1, c)),
            _resident((c, d)),
            pl.BlockSpec((ts, d), lambda b, t: (row(b, t), g_blk)),
        ],
        out_specs=pl.BlockSpec((ts, d), lambda b, t: (row(b, t), 0)),
        out_shape=jax.ShapeDtypeStruct((batch * seq, d), BF16),
        scratch_shapes=[pltpu.VMEM((SUBLANES, ts + CONV_HIST, c), F32), pltpu.VMEM((ts, c), F32)],
        compiler_params=_params("arbitrary", "arbitrary"),
        name="mixer_a",
    )(z_main, z_main, jnp.broadcast_to(conv_w[:, None, :], (width, SUBLANES, c)), conv_b.reshape(1, c),
      conv_norm.reshape(1, c), w_a_out, gates)


def _gelu_tanh(x):
    return 0.5 * x * (1.0 + jnp.tanh(0.7978845608028654 * (x + 0.044715 * (x * x * x))))


def _mixer_b_kernel(u_ref, v_ref, sn_ref, sw_ref, sb_ref, wout_ref, g_ref, yin_ref, o_ref, uv_ref,
                    *, n_blocks, groups, block):
    gd = u_ref.shape[-1] // groups
    v = _rms(_gelu_tanh(v_ref[...].astype(F32)), sn_ref[...]).astype(BF16)
    rc = lax.broadcasted_iota(jnp.int32, (block, block), 0) // CHUNK
    cc = lax.broadcasted_iota(jnp.int32, (block, block), 1) // CHUNK
    causal = rc >= cc
    for g in range(groups):
        w = jnp.where(causal, sw_ref[g], 0.0).astype(BF16)
        bias = sb_ref[:, g:g + 1]
        cols = slice(g * gd, (g + 1) * gd)
        for r in range(n_blocks):
            rows = slice(r * block, (r + 1) * block)
            sv = jnp.dot(w, v[rows, cols], preferred_element_type=F32) + bias
            u = _gelu_tanh(u_ref[rows, cols].astype(F32))
            uv_ref[rows, cols] = (u * sv).astype(BF16)
    out = jnp.dot(uv_ref[...], wout_ref[...], preferred_element_type=F32)
    o_ref[...] = (yin_ref[...].astype(F32) + _sigmoid(g_ref[...].astype(F32)) * out).astype(o_ref.dtype)


def mixer_b(z_main, gates, y_in, sg_norm, sg_w, sg_b, w_b_out, u_blk, v_blk, g_blk, ts=512):
    groups, block, _ = sg_w.shape
    c, d = w_b_out.shape
    t = z_main.shape[0]
    ts = max(_tiles(t, ts), block)
    kern = functools.partial(_mixer_b_kernel, n_blocks=ts // block, groups=groups, block=block)
    return pl.pallas_call(
        kern,
        grid=(t // ts,),
        in_specs=[
            pl.BlockSpec((ts, c), lambda i: (i, u_blk)),
            pl.BlockSpec((ts, c), lambda i: (i, v_blk)),
            _resident((1, c)), _resident((groups, block, block)), _resident((block, groups)),
            _resident((c, d)),
            pl.BlockSpec((ts, d), lambda i: (i, g_blk)),
            pl.BlockSpec((ts, d), lambda i: (i, 0)),
        ],
        out_specs=pl.BlockSpec((ts, d), lambda i: (i, 0)),
        out_shape=jax.ShapeDtypeStruct((t, d), BF16),
        scratch_shapes=[pltpu.VMEM((ts, c), BF16)],
        compiler_params=_params("parallel"),
        name="mixer_b",
    )(z_main, z_main, sg_norm.reshape(1, c), sg_w, jnp.transpose(sg_b), w_b_out, gates, y_in)


def _rope_group(x, cos_ref, sa_ref, sb_ref):
    half = QK_ROPE // 2
    return (x * cos_ref[...] + pltpu.roll(x, LANES - half, 1) * sa_ref[...]
            + pltpu.roll(x, half, 1) * sb_ref[...])


def _norm_group(x, gain, n):
    ms = jnp.sum(x * x, axis=-1, keepdims=True) * (1.0 / n)
    return x * lax.rsqrt(ms + EPS) * gain


def _q_proj_kernel(cq_ref, qn_ref, wq_ref, gn_ref, gr_ref, cos_ref, sa_ref, sb_ref, o_ref, *, heads):
    h = _rms(cq_ref[...].astype(F32), qn_ref[...]).astype(BF16)
    q = jnp.dot(h, wq_ref[...], preferred_element_type=F32)
    scale = QK_DIM ** -0.5 * LOG2_E
    for hd in range(heads):
        c0 = hd * HEAD_PAD
        nope = _norm_group(q[:, c0:c0 + QK_NOPE], gn_ref[...], QK_NOPE)
        rope = _norm_group(q[:, c0 + QK_NOPE:c0 + HEAD_PAD], gr_ref[...], QK_ROPE)
        rope = _rope_group(rope, cos_ref, sa_ref, sb_ref)
        o_ref[:, c0:c0 + QK_NOPE] = (nope * scale).astype(o_ref.dtype)
        o_ref[:, c0 + QK_NOPE:c0 + HEAD_PAD] = (rope * scale).astype(o_ref.dtype)


def _kv_proj_kernel(ckv_ref, kr_ref, kvn_ref, wkv_ref, gn_ref, gr_ref, cos_ref, sa_ref, sb_ref,
                    k_ref, v_ref, *, heads):
    h = _rms(ckv_ref[...].astype(F32), kvn_ref[...]).astype(BF16)
    kv = jnp.dot(h, wkv_ref[...], preferred_element_type=F32)
    kr = kr_ref[...].astype(F32)
    kr = jnp.where(lax.broadcasted_iota(jnp.int32, kr.shape, 1) < QK_ROPE, kr, 0.0)
    rope = _norm_group(kr, gr_ref[...], QK_ROPE)
    rope = _rope_group(rope, cos_ref, sa_ref, sb_ref).astype(k_ref.dtype)
    for hd in range(heads):
        c0 = hd * (QK_NOPE + V_DIM)
        nope = _norm_group(kv[:, c0:c0 + QK_NOPE], gn_ref[...], QK_NOPE)
        k_ref[:, hd * HEAD_PAD:hd * HEAD_PAD + QK_NOPE] = nope.astype(k_ref.dtype)
        k_ref[:, hd * HEAD_PAD + QK_NOPE:(hd + 1) * HEAD_PAD] = rope
        v_ref[:, 2 * hd * V_DIM:(2 * hd + 1) * V_DIM] = kv[:, c0 + QK_NOPE:c0 + QK_NOPE + V_DIM].astype(v_ref.dtype)
        v_ref[:, (2 * hd + 1) * V_DIM:(2 * hd + 2) * V_DIM] = jnp.ones((kv.shape[0], V_DIM), v_ref.dtype)


def _pad_gain(gain):
    gn = gain[:QK_NOPE].reshape(1, QK_NOPE)
    gr = jnp.concatenate([gain[QK_NOPE:], jnp.zeros((LANES - QK_ROPE,), gain.dtype)]).reshape(1, LANES)
    return gn, gr


def q_proj(z_main, cq_blk, q_norm, w_uq_pad, q_gain, tabs, heads, tm=256):
    t = z_main.shape[0]
    ql = q_norm.shape[0]
    tm = _tiles(t, tm)
    gn, gr = _pad_gain(q_gain)
    tab_spec = pl.BlockSpec((tm, LANES), lambda i: (i, 0))
    return pl.pallas_call(
        functools.partial(_q_proj_kernel, heads=heads),
        grid=(t // tm,),
        in_specs=[pl.BlockSpec((tm, ql), lambda i: (i, cq_blk)), _resident((1, ql)),
                  _resident((ql, heads * HEAD_PAD)), _resident((1, QK_NOPE)), _resident((1, LANES)),
                  tab_spec, tab_spec, tab_spec],
        out_specs=pl.BlockSpec((tm, heads * HEAD_PAD), lambda i: (i, 0)),
        out_shape=jax.ShapeDtypeStruct((t, heads * HEAD_PAD), BF16),
        compiler_params=_params("parallel"),
        name="q_proj",
    )(z_main, q_norm.reshape(1, ql), w_uq_pad, gn, gr, *tabs)


def kv_proj(z_main, ckv_blk, k_rope, kv_norm, w_ukv, k_gain, tabs, heads, tm=256):
    t = z_main.shape[0]
    kl = kv_norm.shape[0]
    tm = _tiles(t, tm)
    gn, gr = _pad_gain(k_gain)
    tab_spec = pl.BlockSpec((tm, LANES), lambda i: (i, 0))
    return pl.pallas_call(
        functools.partial(_kv_proj_kernel, heads=heads),
        grid=(t // tm,),
        in_specs=[pl.BlockSpec((tm, kl), lambda i: (i, ckv_blk)), tab_spec, _resident((1, kl)),
                  _resident((kl, heads * (QK_NOPE + V_DIM))), _resident((1, QK_NOPE)), _resident((1, LANES)),
                  tab_spec, tab_spec, tab_spec],
        out_specs=[pl.BlockSpec((tm, heads * HEAD_PAD), lambda i: (i, 0)),
                   pl.BlockSpec((tm, heads * 2 * V_DIM), lambda i: (i, 0))],
        out_shape=[jax.ShapeDtypeStruct((t, heads * HEAD_PAD), BF16),
                   jax.ShapeDtypeStruct((t, heads * 2 * V_DIM), BF16)],
        compiler_params=_params("parallel"),
        name="kv_proj",
    )(z_main, k_rope, kv_norm.reshape(1, kl), w_ukv, gn, gr, *tabs)


def _attn_kernel(q_ref, k_ref, v_ref, o_ref, m_ref, acc_ref, p_ref, a_ref, *, tq, tk, heads_per_step):
    i = pl.program_id(2)
    wv = 2 * V_DIM
    per_q = tq // tk
    assert per_q == 2, "the slot ping-pong below is written for two key tiles per query tile"

    m_ref[...] = jnp.full(m_ref.shape, NEG_BIG, F32)
    acc_ref[...] = jnp.zeros(acc_ref.shape, F32)
    p_ref[1] = jnp.zeros(p_ref.shape[1:], BF16)
    a_ref[1] = jnp.ones(a_ref.shape[1:], F32)

    def scores(j, masked):
        k0 = pl.multiple_of(j * tk, tk)
        out = []
        for hd in range(heads_per_step):
            q = q_ref[:, hd * HEAD_PAD:(hd + 1) * HEAD_PAD]
            k = k_ref[pl.ds(k0, tk), hd * HEAD_PAD:(hd + 1) * HEAD_PAD]
            s = lax.dot_general(q, k, (((1,), (1,)), ((), ())), preferred_element_type=F32)
            if masked:
                q_chunk = (i * tq + lax.broadcasted_iota(jnp.int32, (tq, tk), 0)) // CHUNK
                k_chunk = (k0 + lax.broadcasted_iota(jnp.int32, (tq, tk), 1)) // CHUNK
                s = jnp.where(q_chunk >= k_chunk, s, NEG_BIG)
            out.append(s)
        return out

    def probs(s_heads, slot):
        for hd, s in enumerate(s_heads):
            m_prev = m_ref[hd]
            m_new = jnp.maximum(m_prev, jnp.max(s, axis=-1, keepdims=True))
            p_ref[slot, hd] = jnp.exp2(s - jnp.tile(m_new, (1, tk // LANES))).astype(BF16)
            a_ref[slot, hd] = jnp.exp2(m_prev - m_new)
            m_ref[hd] = m_new

    def accumulate(j, slot):
        k0 = pl.multiple_of(jnp.maximum(j, 0) * tk, tk)
        for hd in range(heads_per_step):
            pv = jnp.dot(p_ref[slot, hd], v_ref[pl.ds(k0, tk), hd * wv:(hd + 1) * wv], preferred_element_type=F32)
            acc_ref[hd] = jnp.tile(a_ref[slot, hd], (1, wv // LANES)) * acc_ref[hd] + pv

    def tile_pair(j, masked):
        accumulate(j - 1, 1)
        s_even = scores(j, masked)
        s_odd = scores(j + 1, masked)
        probs(s_even, 0)
        accumulate(j, 0)
        probs(s_odd, 1)

    def body(t, carry):
        tile_pair(2 * t, False)
        return carry

    lax.fori_loop(0, i, body, 0)
    tile_pair(2 * i, True)
    accumulate(2 * i + 1, 1)
    for hd in range(heads_per_step):
        acc = acc_ref[hd]
        o_ref[:, hd * V_DIM:(hd + 1) * V_DIM] = (acc[:, :V_DIM] / acc[:, V_DIM:V_DIM + 1]).astype(o_ref.dtype)


def attention(q, k, v, batch, seq, heads, tq=1024, heads_per_step=2):
    tq = _tiles(seq, tq)
    tk = tq // 2
    hps = heads_per_step if heads % heads_per_step == 0 else 1
    q3 = q.reshape(batch, seq, heads * HEAD_PAD)
    k3 = k.reshape(batch, seq, heads * HEAD_PAD)
    v3 = v.reshape(batch, seq, heads * 2 * V_DIM)
    out = pl.pallas_call(
        functools.partial(_attn_kernel, tq=tq, tk=tk, heads_per_step=hps),
        grid=(batch, heads // hps, seq // tq),
        in_specs=[pl.BlockSpec((None, tq, hps * HEAD_PAD), lambda b, h, i: (b, i, h)),
                  pl.BlockSpec((None, seq, hps * HEAD_PAD), lambda b, h, i: (b, 0, h)),
                  pl.BlockSpec((None, seq, hps * 2 * V_DIM), lambda b, h, i: (b, 0, h))],
        out_specs=pl.BlockSpec((None, tq, hps * V_DIM), lambda b, h, i: (b, i, h)),
        out_shape=jax.ShapeDtypeStruct((batch, seq, heads * V_DIM), BF16),
        scratch_shapes=[pltpu.VMEM((hps, tq, LANES), F32), pltpu.VMEM((hps, tq, 2 * V_DIM), F32),
                        pltpu.VMEM((2, hps, tq, tk), BF16), pltpu.VMEM((2, hps, tq, LANES), F32)],
        compiler_params=_params("parallel", "parallel", "arbitrary"),
        name="attention",
    )(q3, k3, v3)
    return out.reshape(batch * seq, heads * V_DIM)


def _gated_out_kernel(a_ref, w_ref, g_ref, yin_ref, o_ref):
    out = jnp.dot(a_ref[...], w_ref[...], preferred_element_type=F32)
    o_ref[...] = (yin_ref[...].astype(F32) + _sigmoid(g_ref[...].astype(F32)) * out).astype(o_ref.dtype)


def gated_out(a, w, gates, g_blk, y_in, tm=512):
    t, k = a.shape
    d = w.shape[1]
    tm = _tiles(t, tm)
    return pl.pallas_call(
        _gated_out_kernel,
        grid=(t // tm,),
        in_specs=[pl.BlockSpec((tm, k), lambda i: (i, 0)), _resident((k, d)),
                  pl.BlockSpec((tm, d), lambda i: (i, g_blk)), pl.BlockSpec((tm, d), lambda i: (i, 0))],
        out_specs=pl.BlockSpec((tm, d), lambda i: (i, 0)),
        out_shape=jax.ShapeDtypeStruct((t, d), BF16),
        compiler_params=_params("parallel"),
        name="gated_out",
    )(a, w, gates, y_in)


def _router_kernel(x_ref, g_ref, wr_ref, br_ref, h_ref, w_ref, id_ref, *, n_groups, per_group):
    h = _rms(x_ref[...].astype(F32), g_ref[...])
    h_ref[...] = h.astype(h_ref.dtype)
    logits = jnp.dot(h, wr_ref[...], precision=lax.Precision.HIGHEST, preferred_element_type=F32) + br_ref[...]
    n_exp = n_groups * per_group
    lane = lax.broadcasted_iota(jnp.int32, logits.shape, 1)
    lane_f = lane.astype(F32)
    far = float(LANES)

    def softmax_where(mask):
        z = jnp.where(mask, logits, -jnp.inf)
        e = jnp.exp(z - jnp.max(z, axis=-1, keepdims=True))
        return jnp.where(mask, e / jnp.sum(e, axis=-1, keepdims=True), -1.0)

    def top1(p):
        best = jnp.max(p, axis=-1, keepdims=True)
        idx = jnp.min(jnp.where(p == best, lane_f, far), axis=-1, keepdims=True)
        return best, idx

    g_prob = softmax_where((lane >= n_exp) & (lane < n_exp + n_groups))
    g_w, g_lane = top1(g_prob)
    lo = (g_lane - float(n_exp)) * float(per_group)
    e_prob = softmax_where((lane_f >= lo) & (lane_f < lo + float(per_group)))
    p1, i1 = top1(e_prob)
    p2, i2 = top1(jnp.where(lane_f == i1, -1.0, e_prob))
    den = p1 + p2
    w_ref[...] = jnp.where(lane == 0, g_w * (p1 / den), jnp.where(lane == 1, g_w * (p2 / den), 0.0))
    id_ref[...] = jnp.where(lane == 0, i1, jnp.where(lane == 1, i2, 0.0)).astype(jnp.int32)


def router(x, ffn_norm, w_rg, b_rg, w_re, b_re, tm=256):
    t, d = x.shape
    n_groups, n_exp = w_rg.shape[1], w_re.shape[1]
    tm = _tiles(t, tm)
    pad = jnp.zeros((d, LANES - n_exp - n_groups), F32)
    wr = jnp.concatenate([w_re.astype(F32), w_rg.astype(F32), pad], axis=1)
    br = jnp.concatenate([b_re.astype(F32), b_rg.astype(F32), pad[0]]).reshape(1, LANES)
    h, w, ids = pl.pallas_call(
        functools.partial(_router_kernel, n_groups=n_groups, per_group=n_exp // n_groups),
        grid=(t // tm,),
        in_specs=[pl.BlockSpec((tm, d), lambda i: (i, 0)), _resident((1, d)), _resident((d, LANES)),
                  _resident((1, LANES))],
        out_specs=[pl.BlockSpec((tm, d), lambda i: (i, 0)), pl.BlockSpec((tm, LANES), lambda i: (i, 0)),
                   pl.BlockSpec((tm, LANES), lambda i: (i, 0))],
        out_shape=[jax.ShapeDtypeStruct((t, d), BF16), jax.ShapeDtypeStruct((t, LANES), F32),
                   jax.ShapeDtypeStruct((t, LANES), jnp.int32)],
        compiler_params=_params("parallel"),
        name="router",
    )(x, ffn_norm.reshape(1, d), wr, br)
    return h, w, ids


def _expert_kernel(te_ref, na_ref, x_ref, wg_ref, wu_ref, wd_ref, o_ref):
    del te_ref
    active = pl.program_id(0) < na_ref[0]

    @pl.when(active)
    def _():
        x = x_ref[...]
        g = jnp.dot(x, wg_ref[...], preferred_element_type=F32)
        u = jnp.dot(x, wu_ref[...], preferred_element_type=F32)
        hid = (g * _sigmoid(g) * u).astype(BF16)
        o_ref[...] = jnp.dot(hid, wd_ref[...], preferred_element_type=F32).astype(o_ref.dtype)

    @pl.when(jnp.logical_not(active))
    def _():
        o_ref[...] = jnp.zeros(o_ref.shape, o_ref.dtype)


def expert_ffn(xs, tile_expert, n_active, w_gate, w_up, w_down, tile):
    p, d = xs.shape
    f = w_gate.shape[-1]
    grid_spec = pltpu.PrefetchScalarGridSpec(
        num_scalar_prefetch=2,
        grid=(p // tile,),
        in_specs=[pl.BlockSpec((tile, d), lambda i, te, na: (i, 0)),
                  pl.BlockSpec((None, d, f), lambda i, te, na: (te[i], 0, 0)),
                  pl.BlockSpec((None, d, f), lambda i, te, na: (te[i], 0, 0)),
                  pl.BlockSpec((None, f, d), lambda i, te, na: (te[i], 0, 0))],
        out_specs=pl.BlockSpec((tile, d), lambda i, te, na: (i, 0)),
    )
    return pl.pallas_call(
        _expert_kernel,
        grid_spec=grid_spec,
        out_shape=jax.ShapeDtypeStruct((p, d), BF16),
        compiler_params=_params("arbitrary"),
        name="expert_ffn",
    )(tile_expert, n_active, xs, w_gate, w_up, w_down)


def _combine_kernel(x_ref, w_ref, a_ref, b_ref, o_ref):
    w = w_ref[...]
    o_ref[...] = x_ref[...] + w[:, 0:1] * a_ref[...].astype(F32) + w[:, 1:2] * b_ref[...].astype(F32)


def combine(x, w, r0, r1, tm=256):
    t, d = x.shape
    tm = _tiles(t, tm)
    spec = pl.BlockSpec((tm, d), lambda i: (i, 0))
    return pl.pallas_call(
        _combine_kernel, grid=(t // tm,),
        in_specs=[spec, pl.BlockSpec((tm, LANES), lambda i: (i, 0)), spec, spec], out_specs=spec,
        out_shape=jax.ShapeDtypeStruct((t, d), x.dtype), compiler_params=_params("parallel"), name="combine",
    )(x, w, r0, r1)


def _cast_stream_kernel(x_hbm, o_hbm, ibuf, obuf, isem, osem, *, row0, rows, n_chunks, depth):
    def in_copy(c, slot):
        return pltpu.make_async_copy(x_hbm.at[pl.ds(row0 + c * rows, rows)], ibuf.at[slot], isem.at[slot])

    def out_copy(c, slot):
        return pltpu.make_async_copy(obuf.at[slot], o_hbm.at[pl.ds(c * rows, rows)], osem.at[slot])

    for c in range(min(depth, n_chunks)):
        in_copy(c, c).start()

    def body(c, carry):
        slot = lax.rem(c, depth)
        in_copy(c, slot).wait()

        @pl.when(c >= depth)
        def _():
            out_copy(c - depth, slot).wait()

        obuf[slot] = ibuf[slot].astype(obuf.dtype)
        out_copy(c, slot).start()

        @pl.when(c + depth < n_chunks)
        def _():
            in_copy(c + depth, slot).start()

        return carry

    lax.fori_loop(0, n_chunks, body, 0)
    for c in range(max(n_chunks - depth, 0), n_chunks):
        out_copy(c, c % depth).wait()


def cast_layer(w, layer, out_dtype=BF16):
    per_layer = w.shape[1:]
    b = per_layer[-1]
    n_rows = 1
    for dim in per_layer[:-1]:
        n_rows *= dim
    rows = _tiles(n_rows, max(BF16_SUBLANES, CAST_CHUNK_BYTES // (b * w.dtype.itemsize)))
    kern = functools.partial(_cast_stream_kernel, row0=layer * n_rows, rows=rows, n_chunks=n_rows // rows,
                             depth=CAST_DEPTH)
    out = pl.pallas_call(
        kern,
        in_specs=[pl.BlockSpec(memory_space=pl.ANY)],
        out_specs=pl.BlockSpec(memory_space=pl.ANY),
        out_shape=jax.ShapeDtypeStruct((n_rows, b), out_dtype),
        scratch_shapes=[pltpu.VMEM((CAST_DEPTH, rows, b), w.dtype), pltpu.VMEM((CAST_DEPTH, rows, b), out_dtype),
                        pltpu.SemaphoreType.DMA((CAST_DEPTH,)), pltpu.SemaphoreType.DMA((CAST_DEPTH,))],
        compiler_params=pltpu.CompilerParams(vmem_limit_bytes=VMEM_LIMIT_BYTES),
        name="cast_layer",
    )(w.reshape(w.shape[0] * n_rows, b))
    return out.reshape(per_layer)


def _dispatch_plan(ids, n_exp, tile, n_tiles):
    n_assign = ids.size
    e_flat = ids.reshape(-1)
    iota = jnp.arange(n_assign, dtype=jnp.int32)
    e_sorted, order = lax.sort_key_val(e_flat, iota)
    experts = jnp.arange(n_exp, dtype=jnp.int32)
    counts = jnp.sum((e_flat[:, None] == experts[None, :]).astype(jnp.int32), axis=0)
    starts = jnp.cumsum(counts) - counts
    padded = (counts + tile - 1) // tile * tile
    pad_ends = jnp.cumsum(padded)
    pad_starts = pad_ends - padded
    shift = pad_starts - starts
    dest_sorted = iota + jnp.sum(jnp.where(e_sorted[:, None] == experts[None, :], shift[None, :], 0), axis=1)
    _, pos = lax.sort_key_val(order, dest_sorted)
    n_active = pad_ends[-1] // tile
    tile_idx = jnp.minimum(jnp.arange(n_tiles, dtype=jnp.int32), n_active - 1)
    tile_expert = jnp.sum((pad_ends[None, :] <= (tile_idx * tile)[:, None]).astype(jnp.int32), axis=1)
    tile_expert = jnp.minimum(tile_expert, n_exp - 1)
    slot = jnp.arange(n_tiles * tile, dtype=jnp.int32)
    slot_expert = jnp.repeat(tile_expert, tile)
    rank = slot - pad_starts[slot_expert]
    valid = rank < counts[slot_expert]
    src_assign = order.at[jnp.clip(starts[slot_expert] + rank, 0, n_assign - 1)].get(mode="promise_in_bounds")
    src_tok = jnp.where(valid, src_assign // TOP_K, 0)
    return src_tok, pos.reshape(-1, TOP_K), tile_expert, n_active.reshape(1).astype(jnp.int32)


def _take_rows(x, idx):
    return x.at[idx].get(mode="promise_in_bounds")


def hier_moe(x, ffn_norm, w_rg, b_rg, w_re, b_re, w_gate, w_up, w_down, tile=256):
    t = x.shape[0]
    n_exp = w_gate.shape[0]
    tile = _tiles(t, tile)
    h, wts, ids = router(x, ffn_norm, w_rg, b_rg, w_re, b_re)
    n_tiles = -(-(t * TOP_K + n_exp * (tile - 1)) // tile)
    src_tok, pos, tile_expert, n_active = _dispatch_plan(ids[:, :TOP_K], n_exp, tile, n_tiles)
    rows = expert_ffn(_take_rows(h, src_tok), tile_expert, n_active, w_gate, w_up, w_down, tile)
    return combine(x, wts, _take_rows(rows, pos[:, 0]), _take_rows(rows, pos[:, 1]))


def _rope_tables(positions):
    half = QK_ROPE // 2
    inv = jnp.power(ROPE_BASE, -jnp.arange(0, QK_ROPE, 2, dtype=F32) / QK_ROPE)
    ang = positions.astype(F32)[..., None] * inv
    cos = jnp.cos(ang).reshape(-1, half)
    sin = jnp.sin(ang).reshape(-1, half)
    z = jnp.zeros_like(cos)
    zz = jnp.zeros((cos.shape[0], LANES - QK_ROPE), F32)
    return (jnp.concatenate([cos, cos, zz], axis=1), jnp.concatenate([-sin, z, zz], axis=1),
            jnp.concatenate([z, sin, zz], axis=1))


def mixer_block(x, tabs, batch, seq, layer, mix_norm, w_in_all, conv_w, conv_b, conv_norm, w_a_out_all, sg_norm,
                sg_w, sg_b, w_b_out_all, q_norm, w_uq, kv_norm, w_ukv_all, q_gain, k_gain, w_c_out_all, w_out_all):
    d = x.shape[1]
    c_conv = conv_w.shape[1]
    c_sg = sg_norm.shape[0]
    q_lora, kv_lora = q_norm.shape[0], kv_norm.shape[0]
    heads = w_uq.shape[1] // QK_DIM
    i0 = 2 * c_conv
    i1 = i0 + 2 * c_sg
    i2 = i1 + q_lora
    i3 = i2 + kv_lora
    i4 = i3 + QK_ROPE
    assert c_conv == c_sg and i1 % q_lora == 0 and i2 % kv_lora == 0, "column-block layout of the input projection"

    h = rmsnorm_rows(x, mix_norm)
    w_in_t = jnp.swapaxes(w_in_all, 1, 2)
    z_main = matmul_wt(h, w_in_t, layer, 0, i3, BF16, name="in_proj_main")
    k_rope = matmul_wt(h, w_in_t, layer, i3, LANES, BF16, name="in_proj_rope")
    gates = matmul_wt(h, w_in_t, layer, i4, w_in_t.shape[1] - i4, BF16, name="in_proj_gates")

    y = mixer_a(z_main, gates, conv_w, conv_b, conv_norm, cast_layer(w_a_out_all, layer), batch, seq, 0, 1, 0)
    y = mixer_b(z_main, gates, y, sg_norm, sg_w, sg_b, cast_layer(w_b_out_all, layer), 2, 3, 1)

    w_uq_pad = jnp.pad(w_uq.reshape(q_lora, heads, QK_DIM), ((0, 0), (0, 0), (0, HEAD_PAD - QK_DIM)))
    w_uq_pad = w_uq_pad.reshape(q_lora, heads * HEAD_PAD).astype(BF16)
    q = q_proj(z_main, i1 // q_lora, q_norm, w_uq_pad, q_gain, tabs, heads)
    k, v = kv_proj(z_main, i2 // kv_lora, k_rope, kv_norm, cast_layer(w_ukv_all, layer), k_gain, tabs, heads)
    o = attention(q, k, v, batch, seq, heads)
    y = gated_out(o, cast_layer(w_c_out_all, layer), gates, 2, y)

    return matmul_wf32(y, w_out_all, layer, 0, d, F32, residual=x, name="out_proj")


def kernel(x, positions, mix_norm, w_in, conv_w, conv_b, conv_norm, w_a_out, sg_norm, sg_w, sg_b, w_b_out, q_norm, w_uq, kv_norm, w_ukv, q_gain, k_gain, w_c_out, w_out, ffn_norm, w_router_group, b_router_group, w_router_expert, b_router_expert, w_gate, w_up, w_down):
    batch, seq, d = x.shape
    tabs = _rope_tables(positions)
    xt = x.reshape(batch * seq, d)
    for l in range(mix_norm.shape[0]):
        xt = mixer_block(xt, tabs, batch, seq, l, mix_norm[l], w_in, conv_w[l], conv_b[l], conv_norm[l], w_a_out,
                         sg_norm[l], sg_w[l], sg_b[l], w_b_out, q_norm[l], w_uq[l], kv_norm[l], w_ukv,
                         q_gain[l], k_gain[l], w_c_out, w_out)
        xt = hier_moe(xt, ffn_norm[l], w_router_group[l], b_router_group[l], w_router_expert[l],
                      b_router_expert[l], cast_layer(w_gate, l), cast_layer(w_up, l), cast_layer(w_down, l))
    return xt.reshape(batch, seq, d)
```

```python
import functools

import jax
import jax.numpy as jnp
from jax import lax
from jax.experimental import pallas as pl
from jax.experimental.pallas import tpu as pltpu

CHUNK = 64
EPS = 1e-6
QK_NOPE = 128
QK_ROPE = 64
V_DIM = 128
QK_DIM = QK_NOPE + QK_ROPE
ROPE_BASE = 10000.0
TOP_K = 2

LANES = 128
SUBLANES = 8
BF16_SUBLANES = 16
HEAD_PAD = 2 * LANES
VMEM_LIMIT_BYTES = 56 * 1024 * 1024
CONV_HIST = 32
CAST_CHUNK_BYTES = 1024 * 1024
CAST_DEPTH = 8
NEG_BIG = -1e30
LOG2_E = 1.4426950408889634

F32 = jnp.float32
BF16 = jnp.bfloat16


def _tiles(n, pref):
    if n <= pref:
        return n
    for t in range(pref - pref % BF16_SUBLANES, 0, -BF16_SUBLANES):
        if n % t == 0:
            return t
    return n


def _params(*sem):
    return pltpu.CompilerParams(dimension_semantics=sem, vmem_limit_bytes=VMEM_LIMIT_BYTES)


def _resident(shape):
    nd = len(shape)
    return pl.BlockSpec(shape, lambda *_: (0,) * nd, pipeline_mode=pl.Buffered(1))


def _rms(xf, g):
    ms = jnp.mean(xf * xf, axis=-1, keepdims=True)
    return xf * lax.rsqrt(ms + EPS) * g


def _sigmoid(x):
    return 1.0 / (1.0 + jnp.exp(-x))


def _rmsnorm_kernel(x_ref, g_ref, o_ref):
    o_ref[...] = _rms(x_ref[...].astype(F32), g_ref[...]).astype(o_ref.dtype)


def rmsnorm_rows(x, g, out_dtype=BF16, tm=512):
    t, d = x.shape
    tm = _tiles(t, tm)
    return pl.pallas_call(
        _rmsnorm_kernel,
        grid=(t // tm,),
        in_specs=[pl.BlockSpec((tm, d), lambda i: (i, 0)), _resident((1, d))],
        out_specs=pl.BlockSpec((tm, d), lambda i: (i, 0)),
        out_shape=jax.ShapeDtypeStruct((t, d), out_dtype),
        compiler_params=_params("parallel"),
        name="rmsnorm_rows",
    )(x, g.reshape(1, d))


def _matmul_kernel(a_ref, b_ref, o_ref):
    o_ref[...] = jnp.dot(a_ref[...], b_ref[...], preferred_element_type=F32).astype(o_ref.dtype)


def _matmul_res_kernel(a_ref, b_ref, r_ref, o_ref):
    acc = jnp.dot(a_ref[...], b_ref[...], preferred_element_type=F32)
    o_ref[...] = (r_ref[...].astype(F32) + acc).astype(o_ref.dtype)


def matmul(a, b, out_dtype, residual=None, tm=1024, tn=1024, name="matmul"):
    m, k = a.shape
    n = b.shape[1]
    tm, tn = _tiles(m, tm), _tiles(n, tn)
    in_specs = [pl.BlockSpec((tm, k), lambda i, j: (i, 0)), pl.BlockSpec((k, tn), lambda i, j: (0, j))]
    args = [a, b]
    kern = _matmul_kernel
    if residual is not None:
        in_specs.append(pl.BlockSpec((tm, tn), lambda i, j: (i, j)))
        args.append(residual)
        kern = _matmul_res_kernel
    return pl.pallas_call(
        kern,
        grid=(m // tm, n // tn),
        in_specs=in_specs,
        out_specs=pl.BlockSpec((tm, tn), lambda i, j: (i, j)),
        out_shape=jax.ShapeDtypeStruct((m, n), out_dtype),
        compiler_params=_params("parallel", "arbitrary"),
        name=name,
    )(*args)


def _matmul_wf32_kernel(*refs, shift, k_chunk, has_res):
    a0_ref, a1_ref, w_ref = refs[0], refs[1], refs[2]
    rest = list(refs[3:])
    wn_ref = rest.pop(0) if shift else None
    r_ref = rest.pop(0) if has_res else None
    o_ref, wb_ref = rest
    k = w_ref.shape[0]

    @pl.when(pl.program_id(1) == 0)
    def _():
        for k0 in range(0, k, k_chunk):
            w = w_ref[k0:k0 + k_chunk, :]
            if shift:
                w = jnp.concatenate([w[:, shift:], wn_ref[k0:k0 + k_chunk, :shift]], axis=1)
            wb_ref[k0:k0 + k_chunk, :] = w.astype(BF16)

    kh = a0_ref.shape[1]
    acc = jnp.dot(a0_ref[...], wb_ref[0:kh, :], preferred_element_type=F32)
    acc = acc + jnp.dot(a1_ref[...], wb_ref[kh:, :], preferred_element_type=F32)
    if has_res:
        acc = acc + r_ref[...].astype(F32)
    o_ref[...] = acc.astype(o_ref.dtype)


def matmul_wf32(a, w_all, layer, col0, n, out_dtype, shift=0, residual=None, tm=1024, tn=512, name="matmul_wf32"):
    m, k = a.shape
    tm = _tiles(m, tm)
    tn = max(t for t in range(LANES, max(tn, LANES) + 1, LANES) if n % t == 0 and col0 % t == 0)
    assert 0 <= shift < LANES
    c0 = col0 // tn
    in_specs = [pl.BlockSpec((tm, k // 2), lambda j, i: (i, 0)), pl.BlockSpec((tm, k // 2), lambda j, i: (i, 1)),
                pl.BlockSpec((None, k, tn), lambda j, i: (layer, 0, c0 + j), pipeline_mode=pl.Buffered(1))]
    args = [a, a, w_all]
    if shift:
        in_specs.append(pl.BlockSpec((None, k, LANES), lambda j, i: (layer, 0, (c0 + j + 1) * (tn // LANES)),
                                     pipeline_mode=pl.Buffered(1)))
        args.append(w_all)
    if residual is not None:
        in_specs.append(pl.BlockSpec((tm, tn), lambda j, i: (i, j)))
        args.append(residual)
    kern = functools.partial(_matmul_wf32_kernel, shift=shift, k_chunk=_tiles(k, 512), has_res=residual is not None)
    return pl.pallas_call(
        kern,
        grid=(n // tn, m // tm),
        in_specs=in_specs,
        out_specs=pl.BlockSpec((tm, tn), lambda j, i: (i, j)),
        out_shape=jax.ShapeDtypeStruct((m, n), out_dtype),
        scratch_shapes=[pltpu.VMEM((k, tn), BF16)],
        compiler_params=_params("arbitrary", "arbitrary"),
        name=name,
    )(*args)


def _matmul_wt_kernel(a0_ref, a1_ref, w0_ref, w1_ref, o_ref):
    nt = (((1,), (1,)), ((), ()))
    acc = lax.dot_general(a0_ref[...], w0_ref[0].astype(BF16), nt, preferred_element_type=F32)
    acc = acc + lax.dot_general(a1_ref[...], w1_ref[0].astype(BF16), nt, preferred_element_type=F32)
    o_ref[...] = acc.astype(o_ref.dtype)


def matmul_wt(a, w_t_all, layer, row0, n, out_dtype, tm=2048, tn=512, name="matmul_wt"):
    m, k = a.shape
    tm = _tiles(m, tm)
    tn = max(t for t in range(LANES, max(tn, LANES) + 1, LANES) if n % t == 0)
    assert row0 % SUBLANES == 0 and k % (2 * LANES) == 0
    half = pl.BlockSpec((tm, k // 2), lambda i, j: (i, 0), pipeline_mode=pl.Buffered(1))
    half_hi = pl.BlockSpec((tm, k // 2), lambda i, j: (i, 1), pipeline_mode=pl.Buffered(1))

    def w_half(k0):
        return pl.BlockSpec((pl.Element(1), pl.Element(tn), pl.Element(k // 2)),
                            lambda i, j: (layer, pl.multiple_of(row0 + j * tn, SUBLANES), k0))

    return pl.pallas_call(
        _matmul_wt_kernel,
        grid=(m // tm, n // tn),
        in_specs=[half, half_hi, w_half(0), w_half(k // 2)],
        out_specs=pl.BlockSpec((tm, tn), lambda i, j: (i, j)),
        out_shape=jax.ShapeDtypeStruct((m, n), out_dtype),
        compiler_params=_params("parallel", "arbitrary"),
        name=name,
    )(a, a, w_t_all, w_t_all)


def _mixer_a_kernel(lin_ref, gate_ref, cw_ref, cb_ref, cn_ref, wout_ref, g_ref, o_ref, sh_ref, conv_ref,
                    *, ts, width, rows_per_step):
    c = lin_ref.shape[-1]
    off = CONV_HIST - (width - 1)
    n_shift = sh_ref.shape[1] - SUBLANES

    @pl.when(pl.program_id(1) == 0)
    def _():
        sh_ref[0, 0:CONV_HIST, :] = jnp.zeros((CONV_HIST, c), F32)

    sh_ref[0, CONV_HIST:CONV_HIST + ts, :] = lin_ref[...].astype(F32) * _sigmoid(gate_ref[...].astype(F32))
    for sft in range(1, SUBLANES):
        sh_ref[sft, 0:n_shift, :] = sh_ref[0, sft:sft + n_shift, :]

    for r0 in range(0, ts, rows_per_step):
        acc = jnp.broadcast_to(cb_ref[...], (rows_per_step, c))
        for k in range(width):
            sft, base = (off + k) % SUBLANES, r0 + (off + k) // SUBLANES * SUBLANES
            w_k = jnp.tile(cw_ref[k], (rows_per_step // SUBLANES, 1))
            acc = acc + w_k * sh_ref[sft, base:base + rows_per_step, :]
        conv_ref[r0:r0 + rows_per_step, :] = acc

    sh_ref[0, 0:CONV_HIST, :] = sh_ref[0, ts:ts + CONV_HIST, :]

    y = _rms(conv_ref[...], cn_ref[...])
    y = y * _sigmoid(y)
    out = jnp.dot(y.astype(BF16), wout_ref[...], preferred_element_type=F32)
    o_ref[...] = (_sigmoid(g_ref[...].astype(F32)) * out).astype(o_ref.dtype)


def mixer_a(z_main, gates, conv_w, conv_b, conv_norm, w_a_out, batch, seq, lin_blk, gate_blk, g_blk, ts=256):
    width, c = conv_w.shape
    d = w_a_out.shape[1]
    ts = _tiles(seq, ts)
    nt = seq // ts
    rows_per_step = min(16, ts)
    assert rows_per_step % SUBLANES == 0
    kern = functools.partial(_mixer_a_kernel, ts=ts, width=width, rows_per_step=rows_per_step)
    row = lambda b, t: b * nt + t
    return pl.pallas_call(
        kern,
        grid=(batch, nt),
        in_specs=[
            pl.BlockSpec((ts, c), lambda b, t: (row(b, t), lin_blk)),
            pl.BlockSpec((ts, c), lambda b, t: (row(b, t), gate_blk)),
            _resident((width, SUBLANES, c)), _resident((1, c)), _resident((1, c)),
            _resident((c, d)),
            pl.BlockSpec((ts, d), lambda b, t: (row(b, t), g_blk)),
        ],
        out_specs=pl.BlockSpec((ts, d), lambda b, t: (row(b, t), 0)),
        out_shape=jax.ShapeDtypeStruct((batch * seq, d), BF16),
        scratch_shapes=[pltpu.VMEM((SUBLANES, ts + CONV_HIST, c), F32), pltpu.VMEM((ts, c), F32)],
        compiler_params=_params("arbitrary", "arbitrary"),
        name="mixer_a",
    )(z_main, z_main, jnp.broadcast_to(conv_w[:, None, :], (width, SUBLANES, c)), conv_b.reshape(1, c),
      conv_norm.reshape(1, c), w_a_out, gates)


def _gelu_tanh(x):
    return 0.5 * x * (1.0 + jnp.tanh(0.7978845608028654 * (x + 0.044715 * (x * x * x))))


def _mixer_b_kernel(u_ref, v_ref, sn_ref, sw_ref, sb_ref, wout_ref, g_ref, yin_ref, o_ref, uv_ref,
                    *, n_blocks, groups, block):
    gd = u_ref.shape[-1] // groups
    v = _rms(_gelu_tanh(v_ref[...].astype(F32)), sn_ref[...]).astype(BF16)
    rc = lax.broadcasted_iota(jnp.int32, (block, block), 0) // CHUNK
    cc = lax.broadcasted_iota(jnp.int32, (block, block), 1) // CHUNK
    causal = rc >= cc
    for g in range(groups):
        w = jnp.where(causal, sw_ref[g], 0.0).astype(BF16)
        bias = sb_ref[:, g:g + 1]
        cols = slice(g * gd, (g + 1) * gd)
        for r in range(n_blocks):
            rows = slice(r * block, (r + 1) * block)
            sv = jnp.dot(w, v[rows, cols], preferred_element_type=F32) + bias
            u = _gelu_tanh(u_ref[rows, cols].astype(F32))
            uv_ref[rows, cols] = (u * sv).astype(BF16)
    out = jnp.dot(uv_ref[...], wout_ref[...], preferred_element_type=F32)
    o_ref[...] = (yin_ref[...].astype(F32) + _sigmoid(g_ref[...].astype(F32)) * out).astype(o_ref.dtype)


def mixer_b(z_main, gates, y_in, sg_norm, sg_w, sg_b, w_b_out, u_blk, v_blk, g_blk, ts=512):
    groups, block, _ = sg_w.shape
    c, d = w_b_out.shape
    t = z_main.shape[0]
    ts = max(_tiles(t, ts), block)
    kern = functools.partial(_mixer_b_kernel, n_blocks=ts // block, groups=groups, block=block)
    return pl.pallas_call(
        kern,
        grid=(t // ts,),
        in_specs=[
            pl.BlockSpec((ts, c), lambda i: (i, u_blk)),
            pl.BlockSpec((ts, c), lambda i: (i, v_blk)),
            _resident((1, c)), _resident((groups, block, block)), _resident((block, groups)),
            _resident((c, d)),
            pl.BlockSpec((ts, d), lambda i: (i, g_blk)),
            pl.BlockSpec((ts, d), lambda i: (i, 0)),
        ],
        out_specs=pl.BlockSpec((ts, d), lambda i: (i, 0)),
        out_shape=jax.ShapeDtypeStruct((t, d), BF16),
        scratch_shapes=[pltpu.VMEM((ts, c), BF16)],
        compiler_params=_params("parallel"),
        name="mixer_b",
    )(z_main, z_main, sg_norm.reshape(1, c), sg_w, jnp.transpose(sg_b), w_b_out, gates, y_in)


def _rope_group(x, cos_ref, sa_ref, sb_ref):
    half = QK_ROPE // 2
    return (x * cos_ref[...] + pltpu.roll(x, LANES - half, 1) * sa_ref[...]
            + pltpu.roll(x, half, 1) * sb_ref[...])


def _norm_group(x, gain, n):
    ms = jnp.sum(x * x, axis=-1, keepdims=True) * (1.0 / n)
    return x * lax.rsqrt(ms + EPS) * gain


def _q_proj_kernel(cq_ref, qn_ref, wq_ref, gn_ref, gr_ref, cos_ref, sa_ref, sb_ref, o_ref, *, heads):
    h = _rms(cq_ref[...].astype(F32), qn_ref[...]).astype(BF16)
    q = jnp.dot(h, wq_ref[...], preferred_element_type=F32)
    scale = QK_DIM ** -0.5 * LOG2_E
    for hd in range(heads):
        c0 = hd * HEAD_PAD
        nope = _norm_group(q[:, c0:c0 + QK_NOPE], gn_ref[...], QK_NOPE)
        rope = _norm_group(q[:, c0 + QK_NOPE:c0 + HEAD_PAD], gr_ref[...], QK_ROPE)
        rope = _rope_group(rope, cos_ref, sa_ref, sb_ref)
        o_ref[:, c0:c0 + QK_NOPE] = (nope * scale).astype(o_ref.dtype)
        o_ref[:, c0 + QK_NOPE:c0 + HEAD_PAD] = (rope * scale).astype(o_ref.dtype)


def _kv_proj_kernel(ckv_ref, kr_ref, kvn_ref, wkv_ref, gn_ref, gr_ref, cos_ref, sa_ref, sb_ref,
                    k_ref, v_ref, *, heads):
    h = _rms(ckv_ref[...].astype(F32), kvn_ref[...]).astype(BF16)
    kv = jnp.dot(h, wkv_ref[...], preferred_element_type=F32)
    kr = kr_ref[...].astype(F32)
    kr = jnp.where(lax.broadcasted_iota(jnp.int32, kr.shape, 1) < QK_ROPE, kr, 0.0)
    rope = _norm_group(kr, gr_ref[...], QK_ROPE)
    rope = _rope_group(rope, cos_ref, sa_ref, sb_ref).astype(k_ref.dtype)
    for hd in range(heads):
        c0 = hd * (QK_NOPE + V_DIM)
        nope = _norm_group(kv[:, c0:c0 + QK_NOPE], gn_ref[...], QK_NOPE)
        k_ref[:, hd * HEAD_PAD:hd * HEAD_PAD + QK_NOPE] = nope.astype(k_ref.dtype)
        k_ref[:, hd * HEAD_PAD + QK_NOPE:(hd + 1) * HEAD_PAD] = rope
        v_ref[:, 2 * hd * V_DIM:(2 * hd + 1) * V_DIM] = kv[:, c0 + QK_NOPE:c0 + QK_NOPE + V_DIM].astype(v_ref.dtype)
        v_ref[:, (2 * hd + 1) * V_DIM:(2 * hd + 2) * V_DIM] = jnp.ones((kv.shape[0], V_DIM), v_ref.dtype)


def _pad_gain(gain):
    gn = gain[:QK_NOPE].reshape(1, QK_NOPE)
    gr = jnp.concatenate([gain[QK_NOPE:], jnp.zeros((LANES - QK_ROPE,), gain.dtype)]).reshape(1, LANES)
    return gn, gr


def q_proj(z_main, cq_blk, q_norm, w_uq_pad, q_gain, tabs, heads, tm=256):
    t = z_main.shape[0]
    ql = q_norm.shape[0]
    tm = _tiles(t, tm)
    gn, gr = _pad_gain(q_gain)
    tab_spec = pl.BlockSpec((tm, LANES), lambda i: (i, 0))
    return pl.pallas_call(
        functools.partial(_q_proj_kernel, heads=heads),
        grid=(t // tm,),
        in_specs=[pl.BlockSpec((tm, ql), lambda i: (i, cq_blk)), _resident((1, ql)),
                  _resident((ql, heads * HEAD_PAD)), _resident((1, QK_NOPE)), _resident((1, LANES)),
                  tab_spec, tab_spec, tab_spec],
        out_specs=pl.BlockSpec((tm, heads * HEAD_PAD), lambda i: (i, 0)),
        out_shape=jax.ShapeDtypeStruct((t, heads * HEAD_PAD), BF16),
        compiler_params=_params("parallel"),
        name="q_proj",
    )(z_main, q_norm.reshape(1, ql), w_uq_pad, gn, gr, *tabs)


def kv_proj(z_main, ckv_blk, k_rope, kv_norm, w_ukv, k_gain, tabs, heads, tm=256):
    t = z_main.shape[0]
    kl = kv_norm.shape[0]
    tm = _tiles(t, tm)
    gn, gr = _pad_gain(k_gain)
    tab_spec = pl.BlockSpec((tm, LANES), lambda i: (i, 0))
    return pl.pallas_call(
        functools.partial(_kv_proj_kernel, heads=heads),
        grid=(t // tm,),
        in_specs=[pl.BlockSpec((tm, kl), lambda i: (i, ckv_blk)), tab_spec, _resident((1, kl)),
                  _resident((kl, heads * (QK_NOPE + V_DIM))), _resident((1, QK_NOPE)), _resident((1, LANES)),
                  tab_spec, tab_spec, tab_spec],
        out_specs=[pl.BlockSpec((tm, heads * HEAD_PAD), lambda i: (i, 0)),
                   pl.BlockSpec((tm, heads * 2 * V_DIM), lambda i: (i, 0))],
        out_shape=[jax.ShapeDtypeStruct((t, heads * HEAD_PAD), BF16),
                   jax.ShapeDtypeStruct((t, heads * 2 * V_DIM), BF16)],
        compiler_params=_params("parallel"),
        name="kv_proj",
    )(z_main, k_rope, kv_norm.reshape(1, kl), w_ukv, gn, gr, *tabs)


def _attn_kernel(q_ref, k_ref, v_ref, o_ref, m_ref, acc_ref, p_ref, a_ref, *, tq, tk, heads_per_step):
    i = pl.program_id(2)
    wv = 2 * V_DIM
    per_q = tq // tk
    assert per_q == 2, "the slot ping-pong below is written for two key tiles per query tile"

    m_ref[...] = jnp.full(m_ref.shape, NEG_BIG, F32)
    acc_ref[...] = jnp.zeros(acc_ref.shape, F32)
    p_ref[1] = jnp.zeros(p_ref.shape[1:], BF16)
    a_ref[1] = jnp.ones(a_ref.shape[1:], F32)

    def scores(j, masked):
        k0 = pl.multiple_of(j * tk, tk)
        out = []
        for hd in range(heads_per_step):
            q = q_ref[:, hd * HEAD_PAD:(hd + 1) * HEAD_PAD]
            k = k_ref[pl.ds(k0, tk), hd * HEAD_PAD:(hd + 1) * HEAD_PAD]
            s = lax.dot_general(q, k, (((1,), (1,)), ((), ())), preferred_element_type=F32)
            if masked:
                q_chunk = (i * tq + lax.broadcasted_iota(jnp.int32, (tq, tk), 0)) // CHUNK
                k_chunk = (k0 + lax.broadcasted_iota(jnp.int32, (tq, tk), 1)) // CHUNK
                s = jnp.where(q_chunk >= k_chunk, s, NEG_BIG)
            out.append(s)
        return out

    def probs(s_heads, slot):
        for hd, s in enumerate(s_heads):
            m_prev = m_ref[hd]
            m_new = jnp.maximum(m_prev, jnp.max(s, axis=-1, keepdims=True))
            p_ref[slot, hd] = jnp.exp2(s - jnp.tile(m_new, (1, tk // LANES))).astype(BF16)
            a_ref[slot, hd] = jnp.exp2(m_prev - m_new)
            m_ref[hd] = m_new

    def accumulate(j, slot):
        k0 = pl.multiple_of(jnp.maximum(j, 0) * tk, tk)
        for hd in range(heads_per_step):
            pv = jnp.dot(p_ref[slot, hd], v_ref[pl.ds(k0, tk), hd * wv:(hd + 1) * wv], preferred_element_type=F32)
            acc_ref[hd] = jnp.tile(a_ref[slot, hd], (1, wv // LANES)) * acc_ref[hd] + pv

    def tile_pair(j, masked):
        accumulate(j - 1, 1)
        s_even = scores(j, masked)
        s_odd = scores(j + 1, masked)
        probs(s_even, 0)
        accumulate(j, 0)
        probs(s_odd, 1)

    def body(t, carry):
        tile_pair(2 * t, False)
        return carry

    lax.fori_loop(0, i, body, 0)
    tile_pair(2 * i, True)
    accumulate(2 * i + 1, 1)
    for hd in range(heads_per_step):
        acc = acc_ref[hd]
        o_ref[:, hd * V_DIM:(hd + 1) * V_DIM] = (acc[:, :V_DIM] / acc[:, V_DIM:V_DIM + 1]).astype(o_ref.dtype)


def attention(q, k, v, batch, seq, heads, tq=1024, heads_per_step=2):
    tq = _tiles(seq, tq)
    tk = tq // 2
    hps = heads_per_step if heads % heads_per_step == 0 else 1
    q3 = q.reshape(batch, seq, heads * HEAD_PAD)
    k3 = k.reshape(batch, seq, heads * HEAD_PAD)
    v3 = v.reshape(batch, seq, heads * 2 * V_DIM)
    out = pl.pallas_call(
        functools.partial(_attn_kernel, tq=tq, tk=tk, heads_per_step=hps),
        grid=(batch, heads // hps, seq // tq),
        in_specs=[pl.BlockSpec((None, tq, hps * HEAD_PAD), lambda b, h, i: (b, i, h)),
                  pl.BlockSpec((None, seq, hps * HEAD_PAD), lambda b, h, i: (b, 0, h)),
                  pl.BlockSpec((None, seq, hps * 2 * V_DIM), lambda b, h, i: (b, 0, h))],
        out_specs=pl.BlockSpec((None, tq, hps * V_DIM), lambda b, h, i: (b, i, h)),
        out_shape=jax.ShapeDtypeStruct((batch, seq, heads * V_DIM), BF16),
        scratch_shapes=[pltpu.VMEM((hps, tq, LANES), F32), pltpu.VMEM((hps, tq, 2 * V_DIM), F32),
                        pltpu.VMEM((2, hps, tq, tk), BF16), pltpu.VMEM((2, hps, tq, LANES), F32)],
        compiler_params=_params("parallel", "parallel", "arbitrary"),
        name="attention",
    )(q3, k3, v3)
    return out.reshape(batch * seq, heads * V_DIM)


def _gated_out_kernel(a_ref, w_ref, g_ref, yin_ref, o_ref):
    out = jnp.dot(a_ref[...], w_ref[...], preferred_element_type=F32)
    o_ref[...] = (yin_ref[...].astype(F32) + _sigmoid(g_ref[...].astype(F32)) * out).astype(o_ref.dtype)


def gated_out(a, w, gates, g_blk, y_in, tm=512):
    t, k = a.shape
    d = w.shape[1]
    tm = _tiles(t, tm)
    return pl.pallas_call(
        _gated_out_kernel,
        grid=(t // tm,),
        in_specs=[pl.BlockSpec((tm, k), lambda i: (i, 0)), _resident((k, d)),
                  pl.BlockSpec((tm, d), lambda i: (i, g_blk)), pl.BlockSpec((tm, d), lambda i: (i, 0))],
        out_specs=pl.BlockSpec((tm, d), lambda i: (i, 0)),
        out_shape=jax.ShapeDtypeStruct((t, d), BF16),
        compiler_params=_params("parallel"),
        name="gated_out",
    )(a, w, gates, y_in)


def _router_kernel(x_ref, g_ref, wr_ref, br_ref, h_ref, w_ref, id_ref, *, n_groups, per_group):
    h = _rms(x_ref[...].astype(F32), g_ref[...])
    h_ref[...] = h.astype(h_ref.dtype)
    logits = jnp.dot(h, wr_ref[...], precision=lax.Precision.HIGHEST, preferred_element_type=F32) + br_ref[...]
    n_exp = n_groups * per_group
    lane = lax.broadcasted_iota(jnp.int32, logits.shape, 1)
    lane_f = lane.astype(F32)
    far = float(LANES)

    def softmax_where(mask):
        z = jnp.where(mask, logits, -jnp.inf)
        e = jnp.exp(z - jnp.max(z, axis=-1, keepdims=True))
        return jnp.where(mask, e / jnp.sum(e, axis=-1, keepdims=True), -1.0)

    def top1(p):
        best = jnp.max(p, axis=-1, keepdims=True)
        idx = jnp.min(jnp.where(p == best, lane_f, far), axis=-1, keepdims=True)
        return best, idx

    g_prob = softmax_where((lane >= n_exp) & (lane < n_exp + n_groups))
    g_w, g_lane = top1(g_prob)
    lo = (g_lane - float(n_exp)) * float(per_group)
    e_prob = softmax_where((lane_f >= lo) & (lane_f < lo + float(per_group)))
    p1, i1 = top1(e_prob)
    p2, i2 = top1(jnp.where(lane_f == i1, -1.0, e_prob))
    den = p1 + p2
    w_ref[...] = jnp.where(lane == 0, g_w * (p1 / den), jnp.where(lane == 1, g_w * (p2 / den), 0.0))
    id_ref[...] = jnp.where(lane == 0, i1, jnp.where(lane == 1, i2, 0.0)).astype(jnp.int32)


def router(x, ffn_norm, w_rg, b_rg, w_re, b_re, tm=256):
    t, d = x.shape
    n_groups, n_exp = w_rg.shape[1], w_re.shape[1]
    tm = _tiles(t, tm)
    pad = jnp.zeros((d, LANES - n_exp - n_groups), F32)
    wr = jnp.concatenate([w_re.astype(F32), w_rg.astype(F32), pad], axis=1)
    br = jnp.concatenate([b_re.astype(F32), b_rg.astype(F32), pad[0]]).reshape(1, LANES)
    h, w, ids = pl.pallas_call(
        functools.partial(_router_kernel, n_groups=n_groups, per_group=n_exp // n_groups),
        grid=(t // tm,),
        in_specs=[pl.BlockSpec((tm, d), lambda i: (i, 0)), _resident((1, d)), _resident((d, LANES)),
                  _resident((1, LANES))],
        out_specs=[pl.BlockSpec((tm, d), lambda i: (i, 0)), pl.BlockSpec((tm, LANES), lambda i: (i, 0)),
                   pl.BlockSpec((tm, LANES), lambda i: (i, 0))],
        out_shape=[jax.ShapeDtypeStruct((t, d), BF16), jax.ShapeDtypeStruct((t, LANES), F32),
                   jax.ShapeDtypeStruct((t, LANES), jnp.int32)],
        compiler_params=_params("parallel"),
        name="router",
    )(x, ffn_norm.reshape(1, d), wr, br)
    return h, w, ids


def _expert_kernel(te_ref, na_ref, x_ref, wg_ref, wu_ref, wd_ref, o_ref):
    del te_ref
    active = pl.program_id(0) < na_ref[0]

    @pl.when(active)
    def _():
        x = x_ref[...]
        g = jnp.dot(x, wg_ref[...], preferred_element_type=F32)
        u = jnp.dot(x, wu_ref[...], preferred_element_type=F32)
        hid = (g * _sigmoid(g) * u).astype(BF16)
        o_ref[...] = jnp.dot(hid, wd_ref[...], preferred_element_type=F32).astype(o_ref.dtype)

    @pl.when(jnp.logical_not(active))
    def _():
        o_ref[...] = jnp.zeros(o_ref.shape, o_ref.dtype)


def expert_ffn(xs, tile_expert, n_active, w_gate, w_up, w_down, tile):
    p, d = xs.shape
    f = w_gate.shape[-1]
    grid_spec = pltpu.PrefetchScalarGridSpec(
        num_scalar_prefetch=2,
        grid=(p // tile,),
        in_specs=[pl.BlockSpec((tile, d), lambda i, te, na: (i, 0)),
                  pl.BlockSpec((None, d, f), lambda i, te, na: (te[i], 0, 0)),
                  pl.BlockSpec((None, d, f), lambda i, te, na: (te[i], 0, 0)),
                  pl.BlockSpec((None, f, d), lambda i, te, na: (te[i], 0, 0))],
        out_specs=pl.BlockSpec((tile, d), lambda i, te, na: (i, 0)),
    )
    return pl.pallas_call(
        _expert_kernel,
        grid_spec=grid_spec,
        out_shape=jax.ShapeDtypeStruct((p, d), BF16),
        compiler_params=_params("arbitrary"),
        name="expert_ffn",
    )(tile_expert, n_active, xs, w_gate, w_up, w_down)


def _combine_kernel(x_ref, w_ref, a_ref, b_ref, *rest):
    w = w_ref[...]
    y = x_ref[...] + w[:, 0:1] * a_ref[...].astype(F32) + w[:, 1:2] * b_ref[...].astype(F32)
    if len(rest) == 1:
        rest[0][...] = y
    else:
        g_ref, o_ref, h_ref = rest
        o_ref[...] = y
        h_ref[...] = _rms(y, g_ref[...]).astype(h_ref.dtype)


def combine(x, w, r0, r1, next_norm=None, tm=256):
    t, d = x.shape
    tm = _tiles(t, tm)
    spec = pl.BlockSpec((tm, d), lambda i: (i, 0))
    in_specs = [spec, pl.BlockSpec((tm, LANES), lambda i: (i, 0)), spec, spec]
    args = [x, w, r0, r1]
    out_specs, out_shape = spec, jax.ShapeDtypeStruct((t, d), x.dtype)
    if next_norm is not None:
        in_specs.append(_resident((1, d)))
        args.append(next_norm.reshape(1, d))
        out_specs, out_shape = [spec, spec], [out_shape, jax.ShapeDtypeStruct((t, d), BF16)]
    return pl.pallas_call(
        _combine_kernel, grid=(t // tm,), in_specs=in_specs, out_specs=out_specs, out_shape=out_shape,
        compiler_params=_params("parallel"), name="combine",
    )(*args)


def _cast_stream_kernel(x_hbm, o_hbm, ibuf, obuf, isem, osem, *, row0, rows, n_chunks, depth):
    def in_copy(c, slot):
        return pltpu.make_async_copy(x_hbm.at[pl.ds(row0 + c * rows, rows)], ibuf.at[slot], isem.at[slot])

    def out_copy(c, slot):
        return pltpu.make_async_copy(obuf.at[slot], o_hbm.at[pl.ds(c * rows, rows)], osem.at[slot])

    for c in range(min(depth, n_chunks)):
        in_copy(c, c).start()

    def body(c, carry):
        slot = lax.rem(c, depth)
        in_copy(c, slot).wait()

        @pl.when(c >= depth)
        def _():
            out_copy(c - depth, slot).wait()

        obuf[slot] = ibuf[slot].astype(obuf.dtype)
        out_copy(c, slot).start()

        @pl.when(c + depth < n_chunks)
        def _():
            in_copy(c + depth, slot).start()

        return carry

    lax.fori_loop(0, n_chunks, body, 0)
    for c in range(max(n_chunks - depth, 0), n_chunks):
        out_copy(c, c % depth).wait()


def cast_layer(w, layer, out_dtype=BF16):
    per_layer = w.shape[1:]
    b = per_layer[-1]
    n_rows = 1
    for dim in per_layer[:-1]:
        n_rows *= dim
    rows = _tiles(n_rows, max(BF16_SUBLANES, CAST_CHUNK_BYTES // (b * w.dtype.itemsize)))
    kern = functools.partial(_cast_stream_kernel, row0=layer * n_rows, rows=rows, n_chunks=n_rows // rows,
                             depth=CAST_DEPTH)
    out = pl.pallas_call(
        kern,
        in_specs=[pl.BlockSpec(memory_space=pl.ANY)],
        out_specs=pl.BlockSpec(memory_space=pl.ANY),
        out_shape=jax.ShapeDtypeStruct((n_rows, b), out_dtype),
        scratch_shapes=[pltpu.VMEM((CAST_DEPTH, rows, b), w.dtype), pltpu.VMEM((CAST_DEPTH, rows, b), out_dtype),
                        pltpu.SemaphoreType.DMA((CAST_DEPTH,)), pltpu.SemaphoreType.DMA((CAST_DEPTH,))],
        compiler_params=pltpu.CompilerParams(vmem_limit_bytes=VMEM_LIMIT_BYTES),
        name="cast_layer",
    )(w.reshape(w.shape[0] * n_rows, b))
    return out.reshape(per_layer)


def _dispatch_plan(ids, n_exp, tile, n_tiles):
    n_assign = ids.size
    e_flat = ids.reshape(-1)
    iota = jnp.arange(n_assign, dtype=jnp.int32)
    e_sorted, order = lax.sort_key_val(e_flat, iota)
    experts = jnp.arange(n_exp, dtype=jnp.int32)
    counts = jnp.sum((e_flat[:, None] == experts[None, :]).astype(jnp.int32), axis=0)
    starts = jnp.cumsum(counts) - counts
    padded = (counts + tile - 1) // tile * tile
    pad_ends = jnp.cumsum(padded)
    pad_starts = pad_ends - padded
    shift = pad_starts - starts
    dest_sorted = iota + jnp.sum(jnp.where(e_sorted[:, None] == experts[None, :], shift[None, :], 0), axis=1)
    _, pos = lax.sort_key_val(order, dest_sorted)
    n_active = pad_ends[-1] // tile
    tile_idx = jnp.minimum(jnp.arange(n_tiles, dtype=jnp.int32), n_active - 1)
    tile_expert = jnp.sum((pad_ends[None, :] <= (tile_idx * tile)[:, None]).astype(jnp.int32), axis=1)
    tile_expert = jnp.minimum(tile_expert, n_exp - 1)
    slot = jnp.arange(n_tiles * tile, dtype=jnp.int32)
    slot_expert = jnp.repeat(tile_expert, tile)
    rank = slot - pad_starts[slot_expert]
    valid = rank < counts[slot_expert]
    src_assign = order.at[jnp.clip(starts[slot_expert] + rank, 0, n_assign - 1)].get(mode="promise_in_bounds")
    src_tok = jnp.where(valid, src_assign // TOP_K, 0)
    return src_tok, pos.reshape(-1, TOP_K), tile_expert, n_active.reshape(1).astype(jnp.int32)


def _take_rows(x, idx):
    return x.at[idx].get(mode="promise_in_bounds")


def hier_moe(x, ffn_norm, w_rg, b_rg, w_re, b_re, w_gate, w_up, w_down, next_norm=None, tile=256):
    t = x.shape[0]
    n_exp = w_gate.shape[0]
    tile = _tiles(t, tile)
    h, wts, ids = router(x, ffn_norm, w_rg, b_rg, w_re, b_re)
    n_tiles = -(-(t * TOP_K + n_exp * (tile - 1)) // tile)
    src_tok, pos, tile_expert, n_active = _dispatch_plan(ids[:, :TOP_K], n_exp, tile, n_tiles)
    rows = expert_ffn(_take_rows(h, src_tok), tile_expert, n_active, w_gate, w_up, w_down, tile)
    return combine(x, wts, _take_rows(rows, pos[:, 0]), _take_rows(rows, pos[:, 1]), next_norm)


def _rope_tables(positions):
    half = QK_ROPE // 2
    inv = jnp.power(ROPE_BASE, -jnp.arange(0, QK_ROPE, 2, dtype=F32) / QK_ROPE)
    ang = positions.astype(F32)[..., None] * inv
    cos = jnp.cos(ang).reshape(-1, half)
    sin = jnp.sin(ang).reshape(-1, half)
    z = jnp.zeros_like(cos)
    zz = jnp.zeros((cos.shape[0], LANES - QK_ROPE), F32)
    return (jnp.concatenate([cos, cos, zz], axis=1), jnp.concatenate([-sin, z, zz], axis=1),
            jnp.concatenate([z, sin, zz], axis=1))


def mixer_block(x, h, tabs, batch, seq, layer, mix_norm, w_in_all, conv_w, conv_b, conv_norm, w_a_out_all, sg_norm,
                sg_w, sg_b, w_b_out_all, q_norm, w_uq, kv_norm, w_ukv_all, q_gain, k_gain, w_c_out_all, w_out_all):
    d = x.shape[1]
    c_conv = conv_w.shape[1]
    c_sg = sg_norm.shape[0]
    q_lora, kv_lora = q_norm.shape[0], kv_norm.shape[0]
    heads = w_uq.shape[1] // QK_DIM
    i0 = 2 * c_conv
    i1 = i0 + 2 * c_sg
    i2 = i1 + q_lora
    i3 = i2 + kv_lora
    i4 = i3 + QK_ROPE
    assert c_conv == c_sg and i1 % q_lora == 0 and i2 % kv_lora == 0, "column-block layout of the input projection"

    h = rmsnorm_rows(x, mix_norm) if h is None else h
    w_in_t = jnp.swapaxes(w_in_all, 1, 2)
    z_main = matmul_wt(h, w_in_t, layer, 0, i3, BF16, name="in_proj_main")
    k_rope = matmul_wt(h, w_in_t, layer, i3, LANES, BF16, name="in_proj_rope")
    gates = matmul_wt(h, w_in_t, layer, i4, w_in_t.shape[1] - i4, BF16, name="in_proj_gates")

    y = mixer_a(z_main, gates, conv_w, conv_b, conv_norm, cast_layer(w_a_out_all, layer), batch, seq, 0, 1, 0)
    y = mixer_b(z_main, gates, y, sg_norm, sg_w, sg_b, cast_layer(w_b_out_all, layer), 2, 3, 1)

    w_uq_pad = jnp.pad(w_uq.reshape(q_lora, heads, QK_DIM), ((0, 0), (0, 0), (0, HEAD_PAD - QK_DIM)))
    w_uq_pad = w_uq_pad.reshape(q_lora, heads * HEAD_PAD).astype(BF16)
    q = q_proj(z_main, i1 // q_lora, q_norm, w_uq_pad, q_gain, tabs, heads)
    k, v = kv_proj(z_main, i2 // kv_lora, k_rope, kv_norm, cast_layer(w_ukv_all, layer), k_gain, tabs, heads)
    o = attention(q, k, v, batch, seq, heads)
    y = gated_out(o, cast_layer(w_c_out_all, layer), gates, 2, y)

    return matmul_wf32(y, w_out_all, layer, 0, d, F32, residual=x, name="out_proj")


def kernel(x, positions, mix_norm, w_in, conv_w, conv_b, conv_norm, w_a_out, sg_norm, sg_w, sg_b, w_b_out, q_norm, w_uq, kv_norm, w_ukv, q_gain, k_gain, w_c_out, w_out, ffn_norm, w_router_group, b_router_group, w_router_expert, b_router_expert, w_gate, w_up, w_down):
    batch, seq, d = x.shape
    tabs = _rope_tables(positions)
    xt = x.reshape(batch * seq, d)
    n_layers = mix_norm.shape[0]
    h = None
    for l in range(n_layers):
        xt = mixer_block(xt, h, tabs, batch, seq, l, mix_norm[l], w_in, conv_w[l], conv_b[l], conv_norm[l], w_a_out,
                         sg_norm[l], sg_w[l], sg_b[l], w_b_out, q_norm[l], w_uq[l], kv_norm[l], w_ukv,
                         q_gain[l], k_gain[l], w_c_out, w_out)
        out = hier_moe(xt, ffn_norm[l], w_router_group[l], b_router_group[l], w_router_expert[l],
                       b_router_expert[l], cast_layer(w_gate, l), cast_layer(w_up, l), cast_layer(w_down, l),
                       next_norm=mix_norm[l + 1] if l + 1 < n_layers else None)
        xt, h = out if l + 1 < n_layers else (out, None)
    return xt.reshape(batch, seq, d)
```

```python
import functools

import jax
import jax.numpy as jnp
from jax import lax
from jax.experimental import pallas as pl
from jax.experimental.pallas import tpu as pltpu

CHUNK = 64
EPS = 1e-6
QK_NOPE = 128
QK_ROPE = 64
V_DIM = 128
QK_DIM = QK_NOPE + QK_ROPE
ROPE_BASE = 10000.0
TOP_K = 2

LANES = 128
SUBLANES = 8
BF16_SUBLANES = 16
HEAD_PAD = 2 * LANES
VMEM_LIMIT_BYTES = 56 * 1024 * 1024
CONV_HIST = 32
NEG_BIG = -1e30
LOG2_E = 1.4426950408889634

F32 = jnp.float32
BF16 = jnp.bfloat16


def _tiles(n, pref):
    if n <= pref:
        return n
    for t in range(pref - pref % BF16_SUBLANES, 0, -BF16_SUBLANES):
        if n % t == 0:
            return t
    return n


def _params(*sem):
    return pltpu.CompilerParams(dimension_semantics=sem, vmem_limit_bytes=VMEM_LIMIT_BYTES)


def _resident(shape):
    nd = len(shape)
    return pl.BlockSpec(shape, lambda *_: (0,) * nd, pipeline_mode=pl.Buffered(1))


def _rms(xf, g):
    ms = jnp.mean(xf * xf, axis=-1, keepdims=True)
    return xf * lax.rsqrt(ms + EPS) * g


def _sigmoid(x):
    return 1.0 / (1.0 + jnp.exp(-x))


def _rmsnorm_kernel(x_ref, g_ref, o_ref):
    o_ref[...] = _rms(x_ref[...].astype(F32), g_ref[...]).astype(o_ref.dtype)


def rmsnorm_rows(x, g, out_dtype=BF16, tm=512):
    t, d = x.shape
    tm = _tiles(t, tm)
    return pl.pallas_call(
        _rmsnorm_kernel,
        grid=(t // tm,),
        in_specs=[pl.BlockSpec((tm, d), lambda i: (i, 0)), _resident((1, d))],
        out_specs=pl.BlockSpec((tm, d), lambda i: (i, 0)),
        out_shape=jax.ShapeDtypeStruct((t, d), out_dtype),
        compiler_params=_params("parallel"),
        name="rmsnorm_rows",
    )(x, g.reshape(1, d))


def _matmul_kernel(a_ref, b_ref, o_ref):
    o_ref[...] = jnp.dot(a_ref[...], b_ref[...], preferred_element_type=F32).astype(o_ref.dtype)


def _matmul_res_kernel(a_ref, b_ref, r_ref, o_ref):
    acc = jnp.dot(a_ref[...], b_ref[...], preferred_element_type=F32)
    o_ref[...] = (r_ref[...].astype(F32) + acc).astype(o_ref.dtype)


def matmul(a, b, out_dtype, residual=None, tm=1024, tn=1024, name="matmul"):
    m, k = a.shape
    n = b.shape[1]
    tm, tn = _tiles(m, tm), _tiles(n, tn)
    in_specs = [pl.BlockSpec((tm, k), lambda i, j: (i, 0)), pl.BlockSpec((k, tn), lambda i, j: (0, j))]
    args = [a, b]
    kern = _matmul_kernel
    if residual is not None:
        in_specs.append(pl.BlockSpec((tm, tn), lambda i, j: (i, j)))
        args.append(residual)
        kern = _matmul_res_kernel
    return pl.pallas_call(
        kern,
        grid=(m // tm, n // tn),
        in_specs=in_specs,
        out_specs=pl.BlockSpec((tm, tn), lambda i, j: (i, j)),
        out_shape=jax.ShapeDtypeStruct((m, n), out_dtype),
        compiler_params=_params("parallel", "arbitrary"),
        name=name,
    )(*args)


def _matmul_wf32_kernel(*refs, shift, k_chunk, has_res):
    a0_ref, a1_ref, w_ref = refs[0], refs[1], refs[2]
    rest = list(refs[3:])
    wn_ref = rest.pop(0) if shift else None
    r_ref = rest.pop(0) if has_res else None
    o_ref, wb_ref = rest
    k = w_ref.shape[0]

    @pl.when(pl.program_id(1) == 0)
    def _():
        for k0 in range(0, k, k_chunk):
            w = w_ref[k0:k0 + k_chunk, :]
            if shift:
                w = jnp.concatenate([w[:, shift:], wn_ref[k0:k0 + k_chunk, :shift]], axis=1)
            wb_ref[k0:k0 + k_chunk, :] = w.astype(BF16)

    kh = a0_ref.shape[1]
    acc = jnp.dot(a0_ref[...], wb_ref[0:kh, :], preferred_element_type=F32)
    acc = acc + jnp.dot(a1_ref[...], wb_ref[kh:, :], preferred_element_type=F32)
    if has_res:
        acc = acc + r_ref[...].astype(F32)
    o_ref[...] = acc.astype(o_ref.dtype)


def matmul_wf32(a, w_all, layer, col0, n, out_dtype, shift=0, residual=None, tm=1024, tn=512, name="matmul_wf32"):
    m, k = a.shape
    tm = _tiles(m, tm)
    tn = max(t for t in range(LANES, max(tn, LANES) + 1, LANES) if n % t == 0 and col0 % t == 0)
    assert 0 <= shift < LANES
    c0 = col0 // tn
    in_specs = [pl.BlockSpec((tm, k // 2), lambda j, i: (i, 0)), pl.BlockSpec((tm, k // 2), lambda j, i: (i, 1)),
                pl.BlockSpec((None, k, tn), lambda j, i: (layer, 0, c0 + j), pipeline_mode=pl.Buffered(1))]
    args = [a, a, w_all]
    if shift:
        in_specs.append(pl.BlockSpec((None, k, LANES), lambda j, i: (layer, 0, (c0 + j + 1) * (tn // LANES)),
                                     pipeline_mode=pl.Buffered(1)))
        args.append(w_all)
    if residual is not None:
        in_specs.append(pl.BlockSpec((tm, tn), lambda j, i: (i, j)))
        args.append(residual)
    kern = functools.partial(_matmul_wf32_kernel, shift=shift, k_chunk=_tiles(k, 512), has_res=residual is not None)
    return pl.pallas_call(
        kern,
        grid=(n // tn, m // tm),
        in_specs=in_specs,
        out_specs=pl.BlockSpec((tm, tn), lambda j, i: (i, j)),
        out_shape=jax.ShapeDtypeStruct((m, n), out_dtype),
        scratch_shapes=[pltpu.VMEM((k, tn), BF16)],
        compiler_params=_params("arbitrary", "arbitrary"),
        name=name,
    )(*args)


def _matmul_wt_kernel(a0_ref, a1_ref, w0_ref, w1_ref, o_ref):
    nt = (((1,), (1,)), ((), ()))
    acc = lax.dot_general(a0_ref[...], w0_ref[0].astype(BF16), nt, preferred_element_type=F32)
    acc = acc + lax.dot_general(a1_ref[...], w1_ref[0].astype(BF16), nt, preferred_element_type=F32)
    o_ref[...] = acc.astype(o_ref.dtype)


def matmul_wt(a, w_t_all, layer, row0, n, out_dtype, tm=2048, tn=512, name="matmul_wt"):
    m, k = a.shape
    tm = _tiles(m, tm)
    tn = max(t for t in range(LANES, max(tn, LANES) + 1, LANES) if n % t == 0)
    assert row0 % SUBLANES == 0 and k % (2 * LANES) == 0
    half = pl.BlockSpec((tm, k // 2), lambda i, j: (i, 0), pipeline_mode=pl.Buffered(1))
    half_hi = pl.BlockSpec((tm, k // 2), lambda i, j: (i, 1), pipeline_mode=pl.Buffered(1))

    def w_half(k0):
        return pl.BlockSpec((pl.Element(1), pl.Element(tn), pl.Element(k // 2)),
                            lambda i, j: (layer, pl.multiple_of(row0 + j * tn, SUBLANES), k0))

    return pl.pallas_call(
        _matmul_wt_kernel,
        grid=(m // tm, n // tn),
        in_specs=[half, half_hi, w_half(0), w_half(k // 2)],
        out_specs=pl.BlockSpec((tm, tn), lambda i, j: (i, j)),
        out_shape=jax.ShapeDtypeStruct((m, n), out_dtype),
        compiler_params=_params("parallel", "arbitrary"),
        name=name,
    )(a, a, w_t_all, w_t_all)


def _mixer_a_kernel(lin_ref, gate_ref, cw_ref, cb_ref, cn_ref, wout_ref, g_ref, o_ref, sh_ref, conv_ref,
                    *, ts, width, rows_per_step):
    c = lin_ref.shape[-1]
    off = CONV_HIST - (width - 1)
    n_shift = sh_ref.shape[1] - SUBLANES

    @pl.when(pl.program_id(1) == 0)
    def _():
        sh_ref[0, 0:CONV_HIST, :] = jnp.zeros((CONV_HIST, c), F32)

    sh_ref[0, CONV_HIST:CONV_HIST + ts, :] = lin_ref[...].astype(F32) * _sigmoid(gate_ref[...].astype(F32))
    for sft in range(1, SUBLANES):
        sh_ref[sft, 0:n_shift, :] = sh_ref[0, sft:sft + n_shift, :]

    for r0 in range(0, ts, rows_per_step):
        acc = jnp.broadcast_to(cb_ref[...], (rows_per_step, c))
        for k in range(width):
            sft, base = (off + k) % SUBLANES, r0 + (off + k) // SUBLANES * SUBLANES
            w_k = jnp.tile(cw_ref[k], (rows_per_step // SUBLANES, 1))
            acc = acc + w_k * sh_ref[sft, base:base + rows_per_step, :]
        conv_ref[r0:r0 + rows_per_step, :] = acc

    sh_ref[0, 0:CONV_HIST, :] = sh_ref[0, ts:ts + CONV_HIST, :]

    y = _rms(conv_ref[...], cn_ref[...])
    y = y * _sigmoid(y)
    out = jnp.dot(y.astype(BF16), wout_ref[...], preferred_element_type=F32)
    o_ref[...] = (_sigmoid(g_ref[...].astype(F32)) * out).astype(o_ref.dtype)


def mixer_a(z_main, gates, conv_w, conv_b, conv_norm, w_a_out, batch, seq, lin_blk, gate_blk, g_blk, ts=256):
    width, c = conv_w.shape
    d = w_a_out.shape[1]
    ts = _tiles(seq, ts)
    nt = seq // ts
    rows_per_step = min(16, ts)
    assert rows_per_step % SUBLANES == 0
    kern = functools.partial(_mixer_a_kernel, ts=ts, width=width, rows_per_step=rows_per_step)
    row = lambda b, t: b * nt + t
    return pl.pallas_call(
        kern,
        grid=(batch, nt),
        in_specs=[
            pl.BlockSpec((ts, c), lambda b, t: (row(b, t), lin_blk)),
            pl.BlockSpec((ts, c), lambda b, t: (row(b, t), gate_blk)),
            _resident((width, SUBLANES, c)), _resident((1, c)), _resident((1, c)),
            _resident((c, d)),
            pl.BlockSpec((ts, d), lambda b, t: (row(b, t), g_blk)),
        ],
        out_specs=pl.BlockSpec((ts, d), lambda b, t: (row(b, t), 0)),
        out_shape=jax.ShapeDtypeStruct((batch * seq, d), BF16),
        scratch_shapes=[pltpu.VMEM((SUBLANES, ts + CONV_HIST, c), F32), pltpu.VMEM((ts, c), F32)],
        compiler_params=_params("arbitrary", "arbitrary"),
        name="mixer_a",
    )(z_main, z_main, jnp.broadcast_to(conv_w[:, None, :], (width, SUBLANES, c)), conv_b.reshape(1, c),
      conv_norm.reshape(1, c), w_a_out, gates)


def _gelu_tanh(x):
    return 0.5 * x * (1.0 + jnp.tanh(0.7978845608028654 * (x + 0.044715 * (x * x * x))))


def _mixer_b_kernel(u_ref, v_ref, sn_ref, sw_ref, sb_ref, wout_ref, g_ref, yin_ref, o_ref, uv_ref,
                    *, n_blocks, groups, block):
    gd = u_ref.shape[-1] // groups
    v = _rms(_gelu_tanh(v_ref[...].astype(F32)), sn_ref[...]).astype(BF16)
    rc = lax.broadcasted_iota(jnp.int32, (block, block), 0) // CHUNK
    cc = lax.broadcasted_iota(jnp.int32, (block, block), 1) // CHUNK
    causal = rc >= cc
    for g in range(groups):
        w = jnp.where(causal, sw_ref[g], 0.0).astype(BF16)
        bias = sb_ref[:, g:g + 1]
        cols = slice(g * gd, (g + 1) * gd)
        for r in range(n_blocks):
            rows = slice(r * block, (r + 1) * block)
            sv = jnp.dot(w, v[rows, cols], preferred_element_type=F32) + bias
            u = _gelu_tanh(u_ref[rows, cols].astype(F32))
            uv_ref[rows, cols] = (u * sv).astype(BF16)
    out = jnp.dot(uv_ref[...], wout_ref[...], preferred_element_type=F32)
    o_ref[...] = (yin_ref[...].astype(F32) + _sigmoid(g_ref[...].astype(F32)) * out).astype(o_ref.dtype)


def mixer_b(z_main, gates, y_in, sg_norm, sg_w, sg_b, w_b_out, u_blk, v_blk, g_blk, ts=512):
    groups, block, _ = sg_w.shape
    c, d = w_b_out.shape
    t = z_main.shape[0]
    ts = max(_tiles(t, ts), block)
    kern = functools.partial(_mixer_b_kernel, n_blocks=ts // block, groups=groups, block=block)
    return pl.pallas_call(
        kern,
        grid=(t // ts,),
        in_specs=[
            pl.BlockSpec((ts, c), lambda i: (i, u_blk)),
            pl.BlockSpec((ts, c), lambda i: (i, v_blk)),
            _resident((1, c)), _resident((groups, block, block)), _resident((block, groups)),
            _resident((c, d)),
            pl.BlockSpec((ts, d), lambda i: (i, g_blk)),
            pl.BlockSpec((ts, d), lambda i: (i, 0)),
        ],
        out_specs=pl.BlockSpec((ts, d), lambda i: (i, 0)),
        out_shape=jax.ShapeDtypeStruct((t, d), BF16),
        scratch_shapes=[pltpu.VMEM((ts, c), BF16)],
        compiler_params=_params("parallel"),
        name="mixer_b",
    )(z_main, z_main, sg_norm.reshape(1, c), sg_w, jnp.transpose(sg_b), w_b_out, gates, y_in)


def _rope_group(x, cos_ref, sa_ref, sb_ref):
    half = QK_ROPE // 2
    return (x * cos_ref[...] + pltpu.roll(x, LANES - half, 1) * sa_ref[...]
            + pltpu.roll(x, half, 1) * sb_ref[...])


def _norm_group(x, gain, n):
    ms = jnp.sum(x * x, axis=-1, keepdims=True) * (1.0 / n)
    return x * lax.rsqrt(ms + EPS) * gain


def _q_proj_kernel(cq_ref, qn_ref, wq_ref, gn_ref, gr_ref, cos_ref, sa_ref, sb_ref, o_ref, *, heads):
    h = _rms(cq_ref[...].astype(F32), qn_ref[...]).astype(BF16)
    q = jnp.dot(h, wq_ref[...], preferred_element_type=F32)
    scale = QK_DIM ** -0.5 * LOG2_E
    for hd in range(heads):
        c0 = hd * HEAD_PAD
        nope = _norm_group(q[:, c0:c0 + QK_NOPE], gn_ref[...], QK_NOPE)
        rope = _norm_group(q[:, c0 + QK_NOPE:c0 + HEAD_PAD], gr_ref[...], QK_ROPE)
        rope = _rope_group(rope, cos_ref, sa_ref, sb_ref)
        o_ref[:, c0:c0 + QK_NOPE] = (nope * scale).astype(o_ref.dtype)
        o_ref[:, c0 + QK_NOPE:c0 + HEAD_PAD] = (rope * scale).astype(o_ref.dtype)


def _kv_proj_kernel(ckv_ref, kr_ref, kvn_ref, wkv_ref, gn_ref, gr_ref, cos_ref, sa_ref, sb_ref,
                    k_ref, v_ref, *, heads):
    h = _rms(ckv_ref[...].astype(F32), kvn_ref[...]).astype(BF16)
    kv = jnp.dot(h, wkv_ref[...], preferred_element_type=F32)
    kr = kr_ref[...].astype(F32)
    kr = jnp.where(lax.broadcasted_iota(jnp.int32, kr.shape, 1) < QK_ROPE, kr, 0.0)
    rope = _norm_group(kr, gr_ref[...], QK_ROPE)
    rope = _rope_group(rope, cos_ref, sa_ref, sb_ref).astype(k_ref.dtype)
    for hd in range(heads):
        c0 = hd * (QK_NOPE + V_DIM)
        nope = _norm_group(kv[:, c0:c0 + QK_NOPE], gn_ref[...], QK_NOPE)
        k_ref[:, hd * HEAD_PAD:hd * HEAD_PAD + QK_NOPE] = nope.astype(k_ref.dtype)
        k_ref[:, hd * HEAD_PAD + QK_NOPE:(hd + 1) * HEAD_PAD] = rope
        v_ref[:, 2 * hd * V_DIM:(2 * hd + 1) * V_DIM] = kv[:, c0 + QK_NOPE:c0 + QK_NOPE + V_DIM].astype(v_ref.dtype)
        v_ref[:, (2 * hd + 1) * V_DIM:(2 * hd + 2) * V_DIM] = jnp.ones((kv.shape[0], V_DIM), v_ref.dtype)


def _pad_gain(gain):
    gn = gain[:QK_NOPE].reshape(1, QK_NOPE)
    gr = jnp.concatenate([gain[QK_NOPE:], jnp.zeros((LANES - QK_ROPE,), gain.dtype)]).reshape(1, LANES)
    return gn, gr


def q_proj(z_main, cq_blk, q_norm, w_uq_pad, q_gain, tabs, heads, tm=256):
    t = z_main.shape[0]
    ql = q_norm.shape[0]
    tm = _tiles(t, tm)
    gn, gr = _pad_gain(q_gain)
    tab_spec = pl.BlockSpec((tm, LANES), lambda i: (i, 0))
    return pl.pallas_call(
        functools.partial(_q_proj_kernel, heads=heads),
        grid=(t // tm,),
        in_specs=[pl.BlockSpec((tm, ql), lambda i: (i, cq_blk)), _resident((1, ql)),
                  _resident((ql, heads * HEAD_PAD)), _resident((1, QK_NOPE)), _resident((1, LANES)),
                  tab_spec, tab_spec, tab_spec],
        out_specs=pl.BlockSpec((tm, heads * HEAD_PAD), lambda i: (i, 0)),
        out_shape=jax.ShapeDtypeStruct((t, heads * HEAD_PAD), BF16),
        compiler_params=_params("parallel"),
        name="q_proj",
    )(z_main, q_norm.reshape(1, ql), w_uq_pad, gn, gr, *tabs)


def kv_proj(z_main, ckv_blk, k_rope, kv_norm, w_ukv, k_gain, tabs, heads, tm=256):
    t = z_main.shape[0]
    kl = kv_norm.shape[0]
    tm = _tiles(t, tm)
    gn, gr = _pad_gain(k_gain)
    tab_spec = pl.BlockSpec((tm, LANES), lambda i: (i, 0))
    return pl.pallas_call(
        functools.partial(_kv_proj_kernel, heads=heads),
        grid=(t // tm,),
        in_specs=[pl.BlockSpec((tm, kl), lambda i: (i, ckv_blk)), tab_spec, _resident((1, kl)),
                  _resident((kl, heads * (QK_NOPE + V_DIM))), _resident((1, QK_NOPE)), _resident((1, LANES)),
                  tab_spec, tab_spec, tab_spec],
        out_specs=[pl.BlockSpec((tm, heads * HEAD_PAD), lambda i: (i, 0)),
                   pl.BlockSpec((tm, heads * 2 * V_DIM), lambda i: (i, 0))],
        out_shape=[jax.ShapeDtypeStruct((t, heads * HEAD_PAD), BF16),
                   jax.ShapeDtypeStruct((t, heads * 2 * V_DIM), BF16)],
        compiler_params=_params("parallel"),
        name="kv_proj",
    )(z_main, k_rope, kv_norm.reshape(1, kl), w_ukv, gn, gr, *tabs)


def _attn_kernel(q_ref, k_ref, v_ref, o_ref, m_ref, acc_ref, p_ref, a_ref, *, tq, tk, heads_per_step):
    i = pl.program_id(2)
    wv = 2 * V_DIM
    per_q = tq // tk
    assert per_q == 2, "the slot ping-pong below is written for two key tiles per query tile"

    m_ref[...] = jnp.full(m_ref.shape, NEG_BIG, F32)
    acc_ref[...] = jnp.zeros(acc_ref.shape, F32)
    p_ref[1] = jnp.zeros(p_ref.shape[1:], BF16)
    a_ref[1] = jnp.ones(a_ref.shape[1:], F32)

    def scores(j, masked):
        k0 = pl.multiple_of(j * tk, tk)
        out = []
        for hd in range(heads_per_step):
            q = q_ref[:, hd * HEAD_PAD:(hd + 1) * HEAD_PAD]
            k = k_ref[pl.ds(k0, tk), hd * HEAD_PAD:(hd + 1) * HEAD_PAD]
            s = lax.dot_general(q, k, (((1,), (1,)), ((), ())), preferred_element_type=F32)
            if masked:
                q_chunk = (i * tq + lax.broadcasted_iota(jnp.int32, (tq, tk), 0)) // CHUNK
                k_chunk = (k0 + lax.broadcasted_iota(jnp.int32, (tq, tk), 1)) // CHUNK
                s = jnp.where(q_chunk >= k_chunk, s, NEG_BIG)
            out.append(s)
        return out

    def probs(s_heads, slot):
        for hd, s in enumerate(s_heads):
            m_prev = m_ref[hd]
            m_new = jnp.maximum(m_prev, jnp.max(s, axis=-1, keepdims=True))
            p_ref[slot, hd] = jnp.exp2(s - jnp.tile(m_new, (1, tk // LANES))).astype(BF16)
            a_ref[slot, hd] = jnp.exp2(m_prev - m_new)
            m_ref[hd] = m_new

    def accumulate(j, slot):
        k0 = pl.multiple_of(jnp.maximum(j, 0) * tk, tk)
        for hd in range(heads_per_step):
            pv = jnp.dot(p_ref[slot, hd], v_ref[pl.ds(k0, tk), hd * wv:(hd + 1) * wv], preferred_element_type=F32)
            acc_ref[hd] = jnp.tile(a_ref[slot, hd], (1, wv // LANES)) * acc_ref[hd] + pv

    def tile_pair(j, masked):
        accumulate(j - 1, 1)
        s_even = scores(j, masked)
        s_odd = scores(j + 1, masked)
        probs(s_even, 0)
        accumulate(j, 0)
        probs(s_odd, 1)

    def body(t, carry):
        tile_pair(2 * t, False)
        return carry

    lax.fori_loop(0, i, body, 0)
    tile_pair(2 * i, True)
    accumulate(2 * i + 1, 1)
    for hd in range(heads_per_step):
        acc = acc_ref[hd]
        o_ref[:, hd * V_DIM:(hd + 1) * V_DIM] = (acc[:, :V_DIM] / acc[:, V_DIM:V_DIM + 1]).astype(o_ref.dtype)


def attention(q, k, v, batch, seq, heads, tq=1024, heads_per_step=2):
    tq = _tiles(seq, tq)
    tk = tq // 2
    hps = heads_per_step if heads % heads_per_step == 0 else 1
    q3 = q.reshape(batch, seq, heads * HEAD_PAD)
    k3 = k.reshape(batch, seq, heads * HEAD_PAD)
    v3 = v.reshape(batch, seq, heads * 2 * V_DIM)
    out = pl.pallas_call(
        functools.partial(_attn_kernel, tq=tq, tk=tk, heads_per_step=hps),
        grid=(batch, heads // hps, seq // tq),
        in_specs=[pl.BlockSpec((None, tq, hps * HEAD_PAD), lambda b, h, i: (b, i, h)),
                  pl.BlockSpec((None, seq, hps * HEAD_PAD), lambda b, h, i: (b, 0, h)),
                  pl.BlockSpec((None, seq, hps * 2 * V_DIM), lambda b, h, i: (b, 0, h))],
        out_specs=pl.BlockSpec((None, tq, hps * V_DIM), lambda b, h, i: (b, i, h)),
        out_shape=jax.ShapeDtypeStruct((batch, seq, heads * V_DIM), BF16),
        scratch_shapes=[pltpu.VMEM((hps, tq, LANES), F32), pltpu.VMEM((hps, tq, 2 * V_DIM), F32),
                        pltpu.VMEM((2, hps, tq, tk), BF16), pltpu.VMEM((2, hps, tq, LANES), F32)],
        compiler_params=_params("parallel", "parallel", "arbitrary"),
        name="attention",
    )(q3, k3, v3)
    return out.reshape(batch * seq, heads * V_DIM)


def _gated_out_kernel(a_ref, w_ref, g_ref, yin_ref, o_ref):
    out = jnp.dot(a_ref[...], w_ref[...], preferred_element_type=F32)
    o_ref[...] = (yin_ref[...].astype(F32) + _sigmoid(g_ref[...].astype(F32)) * out).astype(o_ref.dtype)


def gated_out(a, w, gates, g_blk, y_in, tm=512):
    t, k = a.shape
    d = w.shape[1]
    tm = _tiles(t, tm)
    return pl.pallas_call(
        _gated_out_kernel,
        grid=(t // tm,),
        in_specs=[pl.BlockSpec((tm, k), lambda i: (i, 0)), _resident((k, d)),
                  pl.BlockSpec((tm, d), lambda i: (i, g_blk)), pl.BlockSpec((tm, d), lambda i: (i, 0))],
        out_specs=pl.BlockSpec((tm, d), lambda i: (i, 0)),
        out_shape=jax.ShapeDtypeStruct((t, d), BF16),
        compiler_params=_params("parallel"),
        name="gated_out",
    )(a, w, gates, y_in)


def _router_kernel(x_ref, g_ref, wr_ref, br_ref, h_ref, w_ref, id_ref, *, n_groups, per_group):
    h = _rms(x_ref[...].astype(F32), g_ref[...])
    h_ref[...] = h.astype(h_ref.dtype)
    logits = jnp.dot(h, wr_ref[...], precision=lax.Precision.HIGHEST, preferred_element_type=F32) + br_ref[...]
    n_exp = n_groups * per_group
    lane = lax.broadcasted_iota(jnp.int32, logits.shape, 1)
    lane_f = lane.astype(F32)
    far = float(LANES)

    def softmax_where(mask):
        z = jnp.where(mask, logits, -jnp.inf)
        e = jnp.exp(z - jnp.max(z, axis=-1, keepdims=True))
        return jnp.where(mask, e / jnp.sum(e, axis=-1, keepdims=True), -1.0)

    def top1(p):
        best = jnp.max(p, axis=-1, keepdims=True)
        idx = jnp.min(jnp.where(p == best, lane_f, far), axis=-1, keepdims=True)
        return best, idx

    g_prob = softmax_where((lane >= n_exp) & (lane < n_exp + n_groups))
    g_w, g_lane = top1(g_prob)
    lo = (g_lane - float(n_exp)) * float(per_group)
    e_prob = softmax_where((lane_f >= lo) & (lane_f < lo + float(per_group)))
    p1, i1 = top1(e_prob)
    p2, i2 = top1(jnp.where(lane_f == i1, -1.0, e_prob))
    den = p1 + p2
    w_ref[...] = jnp.where(lane == 0, g_w * (p1 / den), jnp.where(lane == 1, g_w * (p2 / den), 0.0))
    id_ref[...] = jnp.where(lane == 0, i1, jnp.where(lane == 1, i2, 0.0)).astype(jnp.int32)


def router(x, ffn_norm, w_rg, b_rg, w_re, b_re, tm=256):
    t, d = x.shape
    n_groups, n_exp = w_rg.shape[1], w_re.shape[1]
    tm = _tiles(t, tm)
    pad = jnp.zeros((d, LANES - n_exp - n_groups), F32)
    wr = jnp.concatenate([w_re.astype(F32), w_rg.astype(F32), pad], axis=1)
    br = jnp.concatenate([b_re.astype(F32), b_rg.astype(F32), pad[0]]).reshape(1, LANES)
    h, w, ids = pl.pallas_call(
        functools.partial(_router_kernel, n_groups=n_groups, per_group=n_exp // n_groups),
        grid=(t // tm,),
        in_specs=[pl.BlockSpec((tm, d), lambda i: (i, 0)), _resident((1, d)), _resident((d, LANES)),
                  _resident((1, LANES))],
        out_specs=[pl.BlockSpec((tm, d), lambda i: (i, 0)), pl.BlockSpec((tm, LANES), lambda i: (i, 0)),
                   pl.BlockSpec((tm, LANES), lambda i: (i, 0))],
        out_shape=[jax.ShapeDtypeStruct((t, d), BF16), jax.ShapeDtypeStruct((t, LANES), F32),
                   jax.ShapeDtypeStruct((t, LANES), jnp.int32)],
        compiler_params=_params("parallel"),
        name="router",
    )(x, ffn_norm.reshape(1, d), wr, br)
    return h, w, ids


def _expert_kernel(te_ref, na_ref, x_ref, wg_ref, wu_ref, wd_ref, o_ref):
    del te_ref
    active = pl.program_id(0) < na_ref[0]

    @pl.when(active)
    def _():
        x = x_ref[...]
        g = jnp.dot(x, wg_ref[...], preferred_element_type=F32)
        u = jnp.dot(x, wu_ref[...], preferred_element_type=F32)
        hid = (g * _sigmoid(g) * u).astype(BF16)
        o_ref[...] = jnp.dot(hid, wd_ref[...], preferred_element_type=F32).astype(o_ref.dtype)

    @pl.when(jnp.logical_not(active))
    def _():
        o_ref[...] = jnp.zeros(o_ref.shape, o_ref.dtype)


def expert_ffn(xs, tile_expert, n_active, w_gate, w_up, w_down, tile, expert0=0):
    p, d = xs.shape
    f = w_gate.shape[-1]
    grid_spec = pltpu.PrefetchScalarGridSpec(
        num_scalar_prefetch=2,
        grid=(p // tile,),
        in_specs=[pl.BlockSpec((tile, d), lambda i, te, na: (i, 0)),
                  pl.BlockSpec((None, d, f), lambda i, te, na: (expert0 + te[i], 0, 0)),
                  pl.BlockSpec((None, d, f), lambda i, te, na: (expert0 + te[i], 0, 0)),
                  pl.BlockSpec((None, f, d), lambda i, te, na: (expert0 + te[i], 0, 0))],
        out_specs=pl.BlockSpec((tile, d), lambda i, te, na: (i, 0)),
    )
    return pl.pallas_call(
        _expert_kernel,
        grid_spec=grid_spec,
        out_shape=jax.ShapeDtypeStruct((p, d), BF16),
        compiler_params=_params("arbitrary"),
        name="expert_ffn",
    )(tile_expert, n_active, xs, w_gate, w_up, w_down)


def _combine_kernel(x_ref, w_ref, a_ref, b_ref, *rest):
    w = w_ref[...]
    y = x_ref[...] + w[:, 0:1] * a_ref[...].astype(F32) + w[:, 1:2] * b_ref[...].astype(F32)
    if len(rest) == 1:
        rest[0][...] = y
    else:
        g_ref, o_ref, h_ref = rest
        o_ref[...] = y
        h_ref[...] = _rms(y, g_ref[...]).astype(h_ref.dtype)


def combine(x, w, r0, r1, next_norm=None, tm=256):
    t, d = x.shape
    tm = _tiles(t, tm)
    spec = pl.BlockSpec((tm, d), lambda i: (i, 0))
    in_specs = [spec, pl.BlockSpec((tm, LANES), lambda i: (i, 0)), spec, spec]
    args = [x, w, r0, r1]
    out_specs, out_shape = spec, jax.ShapeDtypeStruct((t, d), x.dtype)
    if next_norm is not None:
        in_specs.append(_resident((1, d)))
        args.append(next_norm.reshape(1, d))
        out_specs, out_shape = [spec, spec], [out_shape, jax.ShapeDtypeStruct((t, d), BF16)]
    return pl.pallas_call(
        _combine_kernel, grid=(t // tm,), in_specs=in_specs, out_specs=out_specs, out_shape=out_shape,
        compiler_params=_params("parallel"), name="combine",
    )(*args)


def _dispatch_plan(ids, n_exp, tile, n_tiles):
    n_assign = ids.size
    e_flat = ids.reshape(-1)
    iota = jnp.arange(n_assign, dtype=jnp.int32)
    e_sorted, order = lax.sort_key_val(e_flat, iota)
    experts = jnp.arange(n_exp, dtype=jnp.int32)
    counts = jnp.sum((e_flat[:, None] == experts[None, :]).astype(jnp.int32), axis=0)
    starts = jnp.cumsum(counts) - counts
    padded = (counts + tile - 1) // tile * tile
    pad_ends = jnp.cumsum(padded)
    pad_starts = pad_ends - padded
    shift = pad_starts - starts
    dest_sorted = iota + jnp.sum(jnp.where(e_sorted[:, None] == experts[None, :], shift[None, :], 0), axis=1)
    _, pos = lax.sort_key_val(order, dest_sorted)
    n_active = pad_ends[-1] // tile
    tile_idx = jnp.minimum(jnp.arange(n_tiles, dtype=jnp.int32), n_active - 1)
    tile_expert = jnp.sum((pad_ends[None, :] <= (tile_idx * tile)[:, None]).astype(jnp.int32), axis=1)
    tile_expert = jnp.minimum(tile_expert, n_exp - 1)
    slot = jnp.arange(n_tiles * tile, dtype=jnp.int32)
    slot_expert = jnp.repeat(tile_expert, tile)
    rank = slot - pad_starts[slot_expert]
    valid = rank < counts[slot_expert]
    src_assign = order.at[jnp.clip(starts[slot_expert] + rank, 0, n_assign - 1)].get(mode="promise_in_bounds")
    src_tok = jnp.where(valid, src_assign // TOP_K, 0)
    return src_tok, pos.reshape(-1, TOP_K), tile_expert, n_active.reshape(1).astype(jnp.int32)


def _take_rows(x, idx):
    return x.at[idx].get(mode="promise_in_bounds")


def hier_moe(x, ffn_norm, w_rg, b_rg, w_re, b_re, w_gate, w_up, w_down, layer, next_norm=None, tile=256):
    t = x.shape[0]
    n_exp = w_re.shape[1]
    tile = _tiles(t, tile)
    h, wts, ids = router(x, ffn_norm, w_rg, b_rg, w_re, b_re)
    n_tiles = -(-(t * TOP_K + n_exp * (tile - 1)) // tile)
    src_tok, pos, tile_expert, n_active = _dispatch_plan(ids[:, :TOP_K], n_exp, tile, n_tiles)
    rows = expert_ffn(_take_rows(h, src_tok), tile_expert, n_active, w_gate, w_up, w_down, tile, layer * n_exp)
    return combine(x, wts, _take_rows(rows, pos[:, 0]), _take_rows(rows, pos[:, 1]), next_norm)


def _rope_tables(positions):
    half = QK_ROPE // 2
    inv = jnp.power(ROPE_BASE, -jnp.arange(0, QK_ROPE, 2, dtype=F32) / QK_ROPE)
    ang = positions.astype(F32)[..., None] * inv
    cos = jnp.cos(ang).reshape(-1, half)
    sin = jnp.sin(ang).reshape(-1, half)
    z = jnp.zeros_like(cos)
    zz = jnp.zeros((cos.shape[0], LANES - QK_ROPE), F32)
    return (jnp.concatenate([cos, cos, zz], axis=1), jnp.concatenate([-sin, z, zz], axis=1),
            jnp.concatenate([z, sin, zz], axis=1))


def mixer_block(x, h, tabs, batch, seq, layer, mix_norm, w_in_all, conv_w, conv_b, conv_norm, w_a_out_all, sg_norm,
                sg_w, sg_b, w_b_out_all, q_norm, w_uq, kv_norm, w_ukv_all, q_gain, k_gain, w_c_out_all, w_out_all):
    d = x.shape[1]
    c_conv = conv_w.shape[1]
    c_sg = sg_norm.shape[0]
    q_lora, kv_lora = q_norm.shape[0], kv_norm.shape[0]
    heads = w_uq.shape[1] // QK_DIM
    i0 = 2 * c_conv
    i1 = i0 + 2 * c_sg
    i2 = i1 + q_lora
    i3 = i2 + kv_lora
    i4 = i3 + QK_ROPE
    assert c_conv == c_sg and i1 % q_lora == 0 and i2 % kv_lora == 0, "column-block layout of the input projection"

    h = rmsnorm_rows(x, mix_norm) if h is None else h
    w_in_t = jnp.swapaxes(w_in_all, 1, 2)
    z_main = matmul_wt(h, w_in_t, layer, 0, i3, BF16, name="in_proj_main")
    k_rope = matmul_wt(h, w_in_t, layer, i3, LANES, BF16, name="in_proj_rope")
    gates = matmul_wt(h, w_in_t, layer, i4, w_in_t.shape[1] - i4, BF16, name="in_proj_gates")

    y = mixer_a(z_main, gates, conv_w, conv_b, conv_norm, w_a_out_all[layer].astype(BF16), batch, seq, 0, 1, 0)
    y = mixer_b(z_main, gates, y, sg_norm, sg_w, sg_b, w_b_out_all[layer].astype(BF16), 2, 3, 1)

    w_uq_pad = jnp.pad(w_uq.reshape(q_lora, heads, QK_DIM), ((0, 0), (0, 0), (0, HEAD_PAD - QK_DIM)))
    w_uq_pad = w_uq_pad.reshape(q_lora, heads * HEAD_PAD).astype(BF16)
    q = q_proj(z_main, i1 // q_lora, q_norm, w_uq_pad, q_gain, tabs, heads)
    k, v = kv_proj(z_main, i2 // kv_lora, k_rope, kv_norm, w_ukv_all[layer].astype(BF16), k_gain, tabs, heads)
    o = attention(q, k, v, batch, seq, heads)
    y = gated_out(o, w_c_out_all[layer].astype(BF16), gates, 2, y)

    return matmul_wf32(y, w_out_all, layer, 0, d, F32, residual=x, name="out_proj")


def kernel(x, positions, mix_norm, w_in, conv_w, conv_b, conv_norm, w_a_out, sg_norm, sg_w, sg_b, w_b_out, q_norm, w_uq, kv_norm, w_ukv, q_gain, k_gain, w_c_out, w_out, ffn_norm, w_router_group, b_router_group, w_router_expert, b_router_expert, w_gate, w_up, w_down):
    batch, seq, d = x.shape
    tabs = _rope_tables(positions)
    xt = x.reshape(batch * seq, d)
    n_layers = mix_norm.shape[0]
    wg_all = w_gate.astype(BF16).reshape((-1,) + w_gate.shape[2:])
    wu_all = w_up.astype(BF16).reshape((-1,) + w_up.shape[2:])
    wd_all = w_down.astype(BF16).reshape((-1,) + w_down.shape[2:])
    h = None
    for l in range(n_layers):
        xt = mixer_block(xt, h, tabs, batch, seq, l, mix_norm[l], w_in, conv_w[l], conv_b[l], conv_norm[l], w_a_out,
                         sg_norm[l], sg_w[l], sg_b[l], w_b_out, q_norm[l], w_uq[l], kv_norm[l], w_ukv,
                         q_gain[l], k_gain[l], w_c_out, w_out)
        out = hier_moe(xt, ffn_norm[l], w_router_group[l], b_router_group[l], w_router_expert[l],
                       b_router_expert[l], wg_all, wu_all, wd_all, l,
                       next_norm=mix_norm[l + 1] if l + 1 < n_layers else None)
        xt, h = out if l + 1 < n_layers else (out, None)
    return xt.reshape(batch, seq, d)
```
